```python
import math
import jax
import jax.numpy as jnp
from jax import lax
import numpy as np

D_MODEL = 1024
BATCH = 8
SEQ = 2048
DEPTH = 4
DEC_BATCH = 128
DEC_SEQ = 8
PAST_LEN = 16384
PAGE_SIZE = 128

N_MIXERS = 2
N_SSD_LAYERS = (DEPTH + 1) // 2
N_POOL_LAYERS = DEPTH // 2
SSD_EXPAND = 2
D_INNER = SSD_EXPAND * D_MODEL
SSD_HEAD_DIM = 64
SSD_HEADS = D_INNER // SSD_HEAD_DIM
SSD_GROUPS = 8
HEADS_PER_GROUP = SSD_HEADS // SSD_GROUPS
SSD_STATE = 128
CONV_W = 4
CONV_DIM = D_INNER + 2 * SSD_GROUPS * SSD_STATE
IN_DIM = D_INNER + CONV_DIM + SSD_HEADS
SSD_CHUNK = 128
POOL_WINDOWS = (2, 4, 8, 16)
POOL_MAX = max(POOL_WINDOWS)
POOL_GROUP = D_MODEL // len(POOL_WINDOWS)
PEER_HEADS = 8
PEER_N_KEYS = 128
PEER_EXPERTS = PEER_N_KEYS * PEER_N_KEYS
PEER_TOPK = 16
PEER_QUERY_DIM = 256
PEER_HALF = PEER_QUERY_DIM // 2
PEER_TOKEN_BLOCK = 256
PLE_DIM = 256
DEEPNORM_ALPHA = (2 * DEPTH) ** 0.25
DEEPNORM_BETA = (8 * DEPTH) ** -0.25
LN_EPS = 1e-5
RMS_EPS = 1e-5

kernel_name = 'hybrid_ssd_pool_peer_decoder_step'


def layer_norm(x, g, b):
    xf = x.astype(jnp.float32)
    mu = jnp.mean(xf, axis=-1, keepdims=True)
    var = jnp.mean(jnp.square(xf - mu), axis=-1, keepdims=True)
    return ((xf - mu) * lax.rsqrt(var + LN_EPS) * g.astype(jnp.float32) + b.astype(jnp.float32)).astype(x.dtype)


def gated_rms_norm(y, z, w):
    b, l, _ = y.shape
    h = (y * jax.nn.silu(z.astype(jnp.float32))).reshape(b, l, SSD_GROUPS, D_INNER // SSD_GROUPS)
    h = h * lax.rsqrt(jnp.mean(h * h, axis=-1, keepdims=True) + RMS_EPS)
    return (h.reshape(b, l, D_INNER) * w.astype(jnp.float32)).astype(z.dtype)


def ssd_scan(x, dt, a, bm, cm, h0):
    b, l = x.shape[:2]
    cl = SSD_CHUNK if l % SSD_CHUNK == 0 else l
    nc = l // cl

    def to_chunks(t):
        return jnp.moveaxis(t.reshape((b, nc, cl) + t.shape[2:]), 1, 0)

    causal = jnp.tril(jnp.ones((cl, cl), dtype=bool))[None, :, :, None, None]

    def step(h, inp):
        xc, dtc, bc, cc = inp
        cum = jnp.cumsum(dtc * a, axis=1)
        diff = cum[:, :, None] - cum[:, None, :]
        decay = jnp.exp(jnp.where(causal, diff, -jnp.inf))
        cb = jnp.einsum('btgn,bsgn->btsg', cc, bc)
        y = jnp.einsum('btsg,btsgj,bsgj,bsgjp->btgjp', cb, decay, dtc, xc)
        y = y + jnp.einsum('btgn,bgjpn->btgjp', cc, h) * jnp.exp(cum)[..., None]
        tail = jnp.exp(cum[:, -1:] - cum) * dtc
        h_new = h * jnp.exp(cum[:, -1])[..., None, None] + jnp.einsum('bsgj,bsgn,bsgjp->bgjpn', tail, bc, xc)
        return h_new, y

    h_last, ys = lax.scan(step, h0, (to_chunks(x), to_chunks(dt), to_chunks(bm), to_chunks(cm)))
    y = jnp.moveaxis(ys, 0, 1).reshape(x.shape)
    return y, h_last


def ssd_mixer(x, conv_buf, ssm_state, w_in, conv_w, conv_b, dt_bias, a_log, d_skip, norm_w, w_out):
    b, l, _ = x.shape
    zxbcdt = x @ w_in
    z = zxbcdt[..., :D_INNER]
    xbc = zxbcdt[..., D_INNER:D_INNER + CONV_DIM]
    dt_raw = zxbcdt[..., D_INNER + CONV_DIM:]
    xpad = jnp.concatenate([conv_buf.astype(xbc.dtype), xbc], axis=1)
    conv = conv_b
    for k in range(CONV_W):
        conv = conv + xpad[:, k:k + l] * conv_w[k]
    xbc = jax.nn.silu(conv)
    gn = SSD_GROUPS * SSD_STATE
    xs = xbc[..., :D_INNER].reshape(b, l, SSD_GROUPS, HEADS_PER_GROUP, SSD_HEAD_DIM).astype(jnp.float32)
    bm = xbc[..., D_INNER:D_INNER + gn].reshape(b, l, SSD_GROUPS, SSD_STATE).astype(jnp.float32)
    cm = xbc[..., D_INNER + gn:].reshape(b, l, SSD_GROUPS, SSD_STATE).astype(jnp.float32)
    dt = jax.nn.softplus(dt_raw.astype(jnp.float32) + dt_bias.astype(jnp.float32))
    dt = dt.reshape(b, l, SSD_GROUPS, HEADS_PER_GROUP)
    a = -jnp.exp(a_log.astype(jnp.float32)).reshape(SSD_GROUPS, HEADS_PER_GROUP)
    h0 = ssm_state.astype(jnp.float32).reshape(b, SSD_GROUPS, HEADS_PER_GROUP, SSD_HEAD_DIM, SSD_STATE)
    y, h_new = ssd_scan(xs, dt, a, bm, cm, h0)
    y = y + d_skip.astype(jnp.float32).reshape(SSD_GROUPS, HEADS_PER_GROUP, 1) * xs
    y = gated_rms_norm(y.reshape(b, l, D_INNER), z, norm_w)
    new_conv = xpad[:, -(CONV_W - 1):]
    new_ssm = h_new.reshape(b, SSD_HEADS, SSD_HEAD_DIM, SSD_STATE).astype(x.dtype)
    return y @ w_out, new_conv, new_ssm


def pool_mixer(x, buf, start, w_grp, scale):
    b, l, d = x.shape
    xp = jnp.concatenate([buf.astype(x.dtype), x], axis=1)
    xf = xp.astype(jnp.float32)
    cs = jnp.concatenate([jnp.zeros((b, 1, d), jnp.float32), jnp.cumsum(xf, axis=1)], axis=1)
    pos = start + jnp.arange(l)
    tok = xf[:, POOL_MAX - 1:]
    outs = []
    for g, w in enumerate(POOL_WINDOWS):
        sl = slice(g * POOL_GROUP, (g + 1) * POOL_GROUP)
        win_sum = cs[:, POOL_MAX:POOL_MAX + l, sl] - cs[:, POOL_MAX - w:POOL_MAX - w + l, sl]
        count = jnp.minimum(pos + 1, w).astype(jnp.float32)[None, :, None]
        outs.append(jnp.einsum('blc,ce->ble', win_sum / count - tok[..., sl], w_grp[g].astype(jnp.float32)))
    out = (jnp.concatenate(outs, axis=-1) * scale.astype(jnp.float32)).astype(x.dtype)
    return out, xp[:, -(POOL_MAX - 1):]


def peer(x, w_q, sub_keys, u_tab, v_tab):
    b, l, d = x.shape
    n_tok = b * l
    n_blk = -(-n_tok // PEER_TOKEN_BLOCK)
    xt = jnp.pad(x.reshape(n_tok, d), ((0, n_blk * PEER_TOKEN_BLOCK - n_tok), (0, 0)))
    xt = xt.reshape(n_blk, PEER_TOKEN_BLOCK, d)
    n_cand = PEER_TOPK * PEER_TOPK

    def one_block(xb):
        q = (xb @ w_q).reshape(PEER_TOKEN_BLOCK, PEER_HEADS, 2, PEER_HALF)
        s = jnp.einsum('thcd,hckd->thck', q, sub_keys).astype(jnp.float32)
        sv, si = lax.top_k(s, PEER_TOPK)
        cand_s = (sv[:, :, 0, :, None] + sv[:, :, 1, None, :]).reshape(PEER_TOKEN_BLOCK, PEER_HEADS, n_cand)
        cand_i = (si[:, :, 0, :, None] * PEER_N_KEYS + si[:, :, 1, None, :]).reshape(PEER_TOKEN_BLOCK, PEER_HEADS, n_cand)
        top_s, pos = lax.top_k(cand_s, PEER_TOPK)
        idx = jnp.take_along_axis(cand_i, pos, axis=-1)
        gate = jax.nn.softmax(top_s, axis=-1).astype(xb.dtype)
        act = jax.nn.gelu(jnp.einsum('thkd,td->thk', u_tab[idx], xb), approximate=False) * gate
        return jnp.einsum('thk,thkd->td', act, v_tab[idx])

    out = lax.map(one_block, xt)
    return out.reshape(n_blk * PEER_TOKEN_BLOCK, d)[:n_tok].reshape(b, l, d)


def trunk(x, p, ssm_st, conv_st, pool_st, start, prm):
    b = x.shape[0]
    new_ssm, new_conv, new_pool = [], [], []
    for i in range(DEPTH):
        j = i // N_MIXERS
        if i % N_MIXERS == 0:
            conv0 = jnp.zeros((b, CONV_W - 1, CONV_DIM), x.dtype) if conv_st is None else conv_st[j]
            ssm0 = jnp.zeros((b, SSD_HEADS, SSD_HEAD_DIM, SSD_STATE), jnp.float32) if ssm_st is None else ssm_st[j]
            mix, c_new, s_new = ssd_mixer(x, conv0, ssm0, prm['ssd_w_in'][j], prm['ssd_conv_w'][j],
                                          prm['ssd_conv_b'][j], prm['ssd_dt_bias'][j], prm['ssd_a_log'][j],
                                          prm['ssd_d'][j], prm['ssd_norm_w'][j], prm['ssd_w_out'][j])
            new_conv.append(c_new)
            new_ssm.append(s_new)
        else:
            pool0 = jnp.zeros((b, POOL_MAX - 1, D_MODEL), x.dtype) if pool_st is None else pool_st[j]
            mix, p_new = pool_mixer(x, pool0, start, prm['pool_w'][j], prm['pool_scale'][j])
            new_pool.append(p_new)
        h = layer_norm(DEEPNORM_ALPHA * x + mix.astype(x.dtype), prm['ln1_g'][i], prm['ln1_b'][i])
        ffn = peer(h, prm['peer_w_q'][i], prm['peer_keys'][i], prm['peer_u'][i], prm['peer_v'][i])
        h = layer_norm(DEEPNORM_ALPHA * h + ffn, prm['ln2_g'][i], prm['ln2_b'][i])
        gate = jax.nn.sigmoid(h @ prm['ple_gate_w'][i] + prm['ple_gate_b'][i])
        x = h + (p[i] @ prm['ple_w'][i]) * gate
    return x, jnp.stack(new_ssm), jnp.stack(new_conv), jnp.stack(new_pool)


def setup_inputs(seed: int = 0) -> dict:
    key = jax.random.key(seed)
    ks = list(jax.random.split(key, 32))
    f32 = jnp.float32

    def nrm(shape, scale):
        return jax.random.normal(ks.pop(), shape, f32) * scale

    x_prompt = nrm((BATCH, SEQ, D_MODEL), 1.0)
    x_sample = nrm((DEC_BATCH, DEC_SEQ, D_MODEL), 1.0)
    p_prompt = nrm((DEPTH, BATCH, SEQ, PLE_DIM), 1.0)
    p_sample = nrm((DEPTH, DEC_BATCH, DEC_SEQ, PLE_DIM), 1.0)
    state_ssm = nrm((N_SSD_LAYERS, DEC_BATCH, SSD_HEADS, SSD_HEAD_DIM, SSD_STATE), 0.2)
    state_conv = nrm((N_SSD_LAYERS, DEC_BATCH, CONV_W - 1, CONV_DIM), 1.0)
    state_pool = nrm((N_POOL_LAYERS, DEC_BATCH, POOL_MAX - 1, D_MODEL), 1.0)

    ssd_w_in = nrm((N_SSD_LAYERS, D_MODEL, IN_DIM), D_MODEL ** -0.5)
    ssd_conv_w = nrm((N_SSD_LAYERS, CONV_W, CONV_DIM), CONV_W ** -0.5)
    ssd_conv_b = nrm((N_SSD_LAYERS, CONV_DIM), 0.02)
    dt0 = jnp.exp(jax.random.uniform(ks.pop(), (N_SSD_LAYERS, SSD_HEADS), f32, math.log(1e-3), math.log(1e-1)))
    ssd_dt_bias = dt0 + jnp.log(-jnp.expm1(-dt0))
    ssd_a_log = jnp.log(jax.random.uniform(ks.pop(), (N_SSD_LAYERS, SSD_HEADS), f32, 1.0, 16.0))
    ssd_d = 1.0 + nrm((N_SSD_LAYERS, SSD_HEADS), 0.02)
    ssd_norm_w = 1.0 + nrm((N_SSD_LAYERS, D_INNER), 0.02)
    ssd_w_out = nrm((N_SSD_LAYERS, D_INNER, D_MODEL), DEEPNORM_BETA * D_INNER ** -0.5)

    pool_w = nrm((N_POOL_LAYERS, len(POOL_WINDOWS), POOL_GROUP, POOL_GROUP), DEEPNORM_BETA * POOL_GROUP ** -0.5)
    pool_scale = 1.0 + nrm((N_POOL_LAYERS, D_MODEL), 0.02)

    peer_w_q = nrm((DEPTH, D_MODEL, PEER_HEADS * PEER_QUERY_DIM), D_MODEL ** -0.5)
    peer_keys = nrm((DEPTH, PEER_HEADS, 2, PEER_N_KEYS, PEER_HALF), PEER_HALF ** -0.5)
    peer_u = nrm((DEPTH, PEER_EXPERTS, D_MODEL), D_MODEL ** -0.5)
    peer_v = nrm((DEPTH, PEER_EXPERTS, D_MODEL), DEEPNORM_BETA * PEER_HEADS ** -0.5)

    ln1_g = 1.0 + nrm((DEPTH, D_MODEL), 0.02)
    ln1_b = nrm((DEPTH, D_MODEL), 0.02)
    ln2_g = 1.0 + nrm((DEPTH, D_MODEL), 0.02)
    ln2_b = nrm((DEPTH, D_MODEL), 0.02)

    ple_w = nrm((DEPTH, PLE_DIM, D_MODEL), DEEPNORM_BETA * PLE_DIM ** -0.5)
    ple_gate_w = nrm((DEPTH, D_MODEL, D_MODEL), D_MODEL ** -0.5)
    ple_gate_b = nrm((DEPTH, D_MODEL), 0.02)

    return {
        'x_prompt': x_prompt, 'x_sample': x_sample, 'p_prompt': p_prompt, 'p_sample': p_sample,
        'state_ssm': state_ssm, 'state_conv': state_conv, 'state_pool': state_pool,
        'ssd_w_in': ssd_w_in, 'ssd_conv_w': ssd_conv_w, 'ssd_conv_b': ssd_conv_b,
        'ssd_dt_bias': ssd_dt_bias, 'ssd_a_log': ssd_a_log, 'ssd_d': ssd_d,
        'ssd_norm_w': ssd_norm_w, 'ssd_w_out': ssd_w_out,
        'pool_w': pool_w, 'pool_scale': pool_scale,
        'peer_w_q': peer_w_q, 'peer_keys': peer_keys, 'peer_u': peer_u, 'peer_v': peer_v,
        'ln1_g': ln1_g, 'ln1_b': ln1_b, 'ln2_g': ln2_g, 'ln2_b': ln2_b,
        'ple_w': ple_w, 'ple_gate_w': ple_gate_w, 'ple_gate_b': ple_gate_b,
    }


def reference(x_prompt, x_sample, p_prompt, p_sample, state_ssm, state_conv, state_pool,
              ssd_w_in, ssd_conv_w, ssd_conv_b, ssd_dt_bias, ssd_a_log, ssd_d, ssd_norm_w, ssd_w_out,
              pool_w, pool_scale, peer_w_q, peer_keys, peer_u, peer_v,
              ln1_g, ln1_b, ln2_g, ln2_b, ple_w, ple_gate_w, ple_gate_b):
    prm = dict(ssd_w_in=ssd_w_in, ssd_conv_w=ssd_conv_w, ssd_conv_b=ssd_conv_b, ssd_dt_bias=ssd_dt_bias,
               ssd_a_log=ssd_a_log, ssd_d=ssd_d, ssd_norm_w=ssd_norm_w, ssd_w_out=ssd_w_out,
               pool_w=pool_w, pool_scale=pool_scale, peer_w_q=peer_w_q, peer_keys=peer_keys,
               peer_u=peer_u, peer_v=peer_v, ln1_g=ln1_g, ln1_b=ln1_b, ln2_g=ln2_g, ln2_b=ln2_b,
               ple_w=ple_w, ple_gate_w=ple_gate_w, ple_gate_b=ple_gate_b)
    y_prompt, ssm_prompt, conv_prompt, pool_prompt = trunk(x_prompt, p_prompt, None, None, None, 0, prm)
    y_sample, ssm_sample, conv_sample, pool_sample = trunk(x_sample, p_sample, state_ssm, state_conv,
                                                           state_pool, PAST_LEN, prm)
    return (y_prompt, y_sample, ssm_prompt, conv_prompt, pool_prompt, ssm_sample, conv_sample, pool_sample)
```

```python
import functools
import math

import jax
import jax.numpy as jnp
from jax import lax
from jax.experimental import pallas as pl
from jax.experimental.pallas import tpu as pltpu

F32 = jnp.float32
BF16 = jnp.bfloat16

D_MODEL = 1024
DEPTH = 4
N_MIXERS = 2
D_INNER = 2048
SSD_HEAD_DIM = 64
SSD_HEADS = 32
SSD_GROUPS = 8
HEADS_PER_GROUP = 4
SSD_STATE = 128
CONV_W = 4
CONV_DIM = D_INNER + 2 * SSD_GROUPS * SSD_STATE
IN_DIM = D_INNER + CONV_DIM + SSD_HEADS
SSD_CHUNK = 128
POOL_WINDOWS = (2, 4, 8, 16)
POOL_MAX = 16
POOL_GROUP = 256
PEER_HEADS = 8
PEER_N_KEYS = 128
PEER_EXPERTS = PEER_N_KEYS * PEER_N_KEYS
PEER_TOPK = 16
PEER_QUERY_DIM = 256
PEER_HALF = 128
PLE_DIM = 256
DEEPNORM_ALPHA = (2 * DEPTH) ** 0.25
LN_EPS = 1e-5
RMS_EPS = 1e-5
PAST_LEN = 16384

LANES = 128
SUBLANES = 8
VMEM_LIMIT = 48 * 1024 * 1024

IN_PAD = 6272
GROUP_W = HEADS_PER_GROUP * SSD_HEAD_DIM
NEG_BIG = -1e30


def _cparams(sem):
    return pltpu.CompilerParams(dimension_semantics=sem, vmem_limit_bytes=VMEM_LIMIT)


def _sigmoid(x):
    return 1.0 / (1.0 + jnp.exp(-x))


def _mm_kernel(x_ref, w_ref, o_ref):
    x = x_ref[...].astype(BF16)
    w = w_ref[...].astype(BF16)
    o_ref[...] = jnp.dot(x, w, preferred_element_type=F32)


def _mm(x, w, tm, tn, name):
    m, k = x.shape
    _, n = w.shape
    tm = min(tm, m)
    return pl.pallas_call(
        _mm_kernel,
        grid=(m // tm, n // tn),
        in_specs=[pl.BlockSpec((tm, k), lambda i, j: (i, 0)),
                  pl.BlockSpec((k, tn), lambda i, j: (0, j))],
        out_specs=pl.BlockSpec((tm, tn), lambda i, j: (i, j)),
        out_shape=jax.ShapeDtypeStruct((m, n), F32),
        compiler_params=_cparams(("parallel", "parallel")),
        name=name,
    )(x, w)


def _resid_ln_kernel(x_ref, y_ref, g_ref, b_ref, o_ref):
    v = DEEPNORM_ALPHA * x_ref[...] + y_ref[...]
    mu = jnp.mean(v, axis=-1, keepdims=True)
    d = v - mu
    var = jnp.mean(d * d, axis=-1, keepdims=True)
    o_ref[...] = d * lax.rsqrt(var + LN_EPS) * g_ref[...] + b_ref[...]


def _resid_ln(x, y, g, b, tm=512):
    m, d = x.shape
    tm = min(tm, m)
    row = pl.BlockSpec((tm, d), lambda i: (i, 0))
    vec = pl.BlockSpec((1, d), lambda i: (0, 0))
    return pl.pallas_call(
        _resid_ln_kernel,
        grid=(m // tm,),
        in_specs=[row, row, vec, vec],
        out_specs=row,
        out_shape=jax.ShapeDtypeStruct((m, d), F32),
        compiler_params=_cparams(("parallel",)),
        name="resid_ln",
    )(x, y, g.reshape(1, d), b.reshape(1, d))


def _conv_kernel(*refs, seq, has_state):
    if has_state:
        x_ref, st_ref, w_ref, b_ref, o_ref = refs
    else:
        x_ref, w_ref, b_ref, o_ref = refs
    x = x_ref[...]
    rows = x.shape[0]
    t = lax.broadcasted_iota(jnp.int32, x.shape, 0) & (seq - 1)
    acc = b_ref[...] + w_ref[CONV_W - 1:CONV_W, :] * x
    for k in range(1, CONV_W):
        tap = pltpu.roll(x, k, 0)
        if has_state:
            prev = pltpu.roll(st_ref[...], rows - (SUBLANES - k), 0)
            tap = jnp.where(t >= k, tap, prev)
        else:
            tap = jnp.where(t >= k, tap, 0.0)
        acc = acc + w_ref[CONV_W - 1 - k:CONV_W - k, :] * tap
    o_ref[...] = acc * _sigmoid(acc)


def _conv(zx, conv_w, conv_b, state8, batch, seq, cb=256):
    has_state = state8 is not None
    col0 = D_INNER // cb
    if has_state:
        assert seq == SUBLANES
        bb = min(batch, 32)
        rows = bb * seq
        grid = (batch // bb, CONV_DIM // cb)
    else:
        rows = seq
        grid = (batch, CONV_DIM // cb)
    in_specs = [pl.BlockSpec((rows, cb), lambda i, c: (i, col0 + c))]
    args = [zx]
    if has_state:
        in_specs.append(pl.BlockSpec((rows, cb), lambda i, c: (i, c)))
        args.append(state8)
    in_specs += [pl.BlockSpec((CONV_W, cb), lambda i, c: (0, c)),
                 pl.BlockSpec((1, cb), lambda i, c: (0, c))]
    args += [conv_w, conv_b.reshape(1, CONV_DIM)]
    return pl.pallas_call(
        functools.partial(_conv_kernel, seq=seq, has_state=has_state),
        grid=grid,
        in_specs=in_specs,
        out_specs=pl.BlockSpec((rows, cb), lambda i, c: (i, c)),
        out_shape=jax.ShapeDtypeStruct((batch * seq, CONV_DIM), F32),
        compiler_params=_cparams(("parallel", "parallel")),
        name="ssd_conv",
    )(*args)


def _dt_kernel(raw_ref, bias_ref, alog_ref, dt_ref, cum_ref, *, seg):
    v = raw_ref[...] + bias_ref[...]
    dt = jnp.maximum(v, 0.0) + jnp.log(1.0 + jnp.exp(-jnp.abs(v)))
    da = dt * (-jnp.exp(alog_ref[...]))
    t = lax.broadcasted_iota(jnp.int32, da.shape, 0) & (seg - 1)
    cum = da
    s = 1
    while s < seg:
        cum = cum + jnp.where(t >= s, pltpu.roll(cum, s, 0), 0.0)
        s *= 2
    dt_ref[...] = dt
    cum_ref[...] = cum


def _dt_prep(zx, dt_bias, a_log, seg):
    m = zx.shape[0]
    rows = min(m, 1024)
    pad = LANES - SSD_HEADS
    bias = jnp.pad(dt_bias, (0, pad)).reshape(1, LANES)
    alog = jnp.pad(a_log, (0, pad)).reshape(1, LANES)
    col = (D_INNER + CONV_DIM) // LANES
    blk = pl.BlockSpec((rows, LANES), lambda i: (i, 0))
    vec = pl.BlockSpec((1, LANES), lambda i: (0, 0))
    return pl.pallas_call(
        functools.partial(_dt_kernel, seg=seg),
        grid=(m // rows,),
        in_specs=[pl.BlockSpec((rows, LANES), lambda i: (i, col)), vec, vec],
        out_specs=[blk, blk],
        out_shape=[jax.ShapeDtypeStruct((m, LANES), F32)] * 2,
        compiler_params=_cparams(("parallel",)),
        name="ssd_dt",
    )(zx, bias, alog)


def _expand_heads(cols, shape):
    hd = lax.broadcasted_iota(jnp.int32, shape, len(shape) - 1) // SSD_HEAD_DIM
    out = jnp.broadcast_to(cols[HEADS_PER_GROUP - 1], shape)
    for j in range(HEADS_PER_GROUP - 2, -1, -1):
        out = jnp.where(hd == j, cols[j], out)
    return out


def _scan_kernel(*refs, lv, has_h0):
    if has_h0:
        (x_ref, b_ref, c_ref, z_ref, dtc_ref, cumt_ref, d_ref, nw_ref, h0_ref, yn_ref, h_ref) = refs
    else:
        (x_ref, b_ref, c_ref, z_ref, dtc_ref, cumt_ref, d_ref, nw_ref, yn_ref, h_ref) = refs
    cl = SSD_CHUNK
    chunk = pl.program_id(2)

    @pl.when(chunk == 0)
    def _():
        if has_h0:
            h_ref[...] = h0_ref[...]
        else:
            h_ref[...] = jnp.zeros_like(h_ref)

    def pad_rows(v):
        if lv == cl:
            return v
        return jnp.concatenate([v, jnp.zeros((cl - lv, v.shape[1]), v.dtype)], axis=0)

    xs = pad_rows(x_ref[...])
    bm = pad_rows(b_ref[...])
    cm = pad_rows(c_ref[...])
    dtc = pad_rows(dtc_ref[0])
    cumt = cumt_ref[0, 0]
    h = h_ref[0]

    row = lax.broadcasted_iota(jnp.int32, (cl, cl), 0)
    col = lax.broadcasted_iota(jnp.int32, (cl, cl), 1)
    valid = row >= col
    if lv < cl:
        valid = valid & (row < lv)

    dt_cols = [dtc[:, j:j + 1] for j in range(HEADS_PER_GROUP)]
    cum_cols = [dtc[:, HEADS_PER_GROUP + j:HEADS_PER_GROUP + j + 1] for j in range(HEADS_PER_GROUP)]
    if lv < cl:
        rmask = lax.broadcasted_iota(jnp.int32, (cl, 1), 0) < lv
        dt_cols = [jnp.where(rmask, v, 0.0) for v in dt_cols]
    last_cols = [v[lv - 1:lv, :] for v in cum_cols]

    wide = (cl, GROUP_W)
    dt_e = _expand_heads(dt_cols, wide)
    cum_e = _expand_heads(cum_cols, wide)
    last_e = _expand_heads(last_cols, wide)
    if lv < cl:
        cum_e = jnp.where(lax.broadcasted_iota(jnp.int32, wide, 0) < lv, cum_e, last_e)

    cb = lax.dot_general(cm.astype(BF16), bm.astype(BF16), (((1,), (1,)), ((), ())),
                         preferred_element_type=F32)
    xdt = xs * dt_e
    hd_w = lax.broadcasted_iota(jnp.int32, wide, 1) // SSD_HEAD_DIM
    m_parts, x_parts = [], []
    for j in range(HEADS_PER_GROUP):
        diff = cum_cols[j] - cumt[j:j + 1, :]
        decay = jnp.exp(jnp.where(valid, diff, NEG_BIG))
        m_parts.append((cb * decay).astype(BF16))
        x_parts.append(jnp.where(hd_w == j, xdt, 0.0).astype(BF16))
    m_cat = jnp.concatenate(m_parts, axis=1)
    x_cat = jnp.concatenate(x_parts, axis=0)
    y = jnp.dot(m_cat, x_cat, preferred_element_type=F32)

    y_state = lax.dot_general(cm.astype(BF16), h.astype(BF16), (((1,), (1,)), ((), ())),
                              preferred_element_type=F32)
    y = y + y_state * jnp.exp(cum_e)
    d_cols = [d_ref[0][:, j:j + 1] for j in range(HEADS_PER_GROUP)]
    y = y + _expand_heads(d_cols, (1, GROUP_W)) * xs

    tail_e = jnp.exp(last_e - cum_e) * dt_e
    w = (xs * tail_e).T
    h_upd = jnp.dot(w.astype(BF16), bm.astype(BF16), preferred_element_type=F32)
    hrow = lax.broadcasted_iota(jnp.int32, (GROUP_W, SSD_STATE), 0) // SSD_HEAD_DIM
    scale = jnp.broadcast_to(jnp.exp(last_cols[HEADS_PER_GROUP - 1]), (GROUP_W, SSD_STATE))
    for j in range(HEADS_PER_GROUP - 2, -1, -1):
        scale = jnp.where(hrow == j, jnp.exp(last_cols[j]), scale)
    h_ref[0] = h * scale + h_upd

    z = z_ref[...]
    g = y[:lv] * (z * _sigmoid(z))
    ms = jnp.mean(g * g, axis=-1, keepdims=True)
    yn_ref[...] = g * lax.rsqrt(ms + RMS_EPS) * nw_ref[...]


def _scan(xbc, zx, dtc, cumt, d_skip, norm_w, h0, batch, seq):
    lv = min(seq, SSD_CHUNK)
    nc = seq // lv
    has_h0 = h0 is not None
    g_w = GROUP_W // LANES
    b_col = D_INNER // SSD_STATE
    c_col = b_col + SSD_GROUPS
    rowblk = lambda b, g, c: b * nc + c
    in_specs = [
        pl.BlockSpec((lv, GROUP_W), lambda b, g, c: (rowblk(b, g, c), g)),
        pl.BlockSpec((lv, SSD_STATE), lambda b, g, c: (rowblk(b, g, c), b_col + g)),
        pl.BlockSpec((lv, SSD_STATE), lambda b, g, c: (rowblk(b, g, c), c_col + g)),
        pl.BlockSpec((lv, GROUP_W), lambda b, g, c: (rowblk(b, g, c), g)),
        pl.BlockSpec((1, lv, SUBLANES), lambda b, g, c: (g, rowblk(b, g, c), 0)),
        pl.BlockSpec((1, 1, SUBLANES, SSD_CHUNK), lambda b, g, c: (g, rowblk(b, g, c), 0, 0)),
        pl.BlockSpec((1, 1, HEADS_PER_GROUP), lambda b, g, c: (g, 0, 0)),
        pl.BlockSpec((1, GROUP_W), lambda b, g, c: (0, g)),
    ]
    del g_w
    args = [xbc, xbc, xbc, zx, dtc, cumt,
            d_skip.reshape(SSD_GROUPS, 1, HEADS_PER_GROUP), norm_w.reshape(1, D_INNER)]
    state_spec = pl.BlockSpec((1, GROUP_W, SSD_STATE), lambda b, g, c: (b * SSD_GROUPS + g, 0, 0))
    if has_h0:
        in_specs.append(state_spec)
        args.append(h0)
    return pl.pallas_call(
        functools.partial(_scan_kernel, lv=lv, has_h0=has_h0),
        grid=(batch, SSD_GROUPS, nc),
        in_specs=in_specs,
        out_specs=[pl.BlockSpec((lv, GROUP_W), lambda b, g, c: (rowblk(b, g, c), g)), state_spec],
        out_shape=[jax.ShapeDtypeStruct((batch * seq, D_INNER), F32),
                   jax.ShapeDtypeStruct((batch * SSD_GROUPS, GROUP_W, SSD_STATE), F32)],
        compiler_params=_cparams(("parallel", "parallel", "arbitrary")),
        name="ssd_scan",
    )(*args)


def _ssd_mixer(x2, batch, seq, conv_state, ssm_state, w_in_bf, conv_w, conv_b, dt_bias, a_log,
               d_skip, norm_w, w_out_bf):
    m = batch * seq
    zx = _mm(x2, w_in_bf, 1024, 896, "ssd_in_proj")
    state8 = None
    if conv_state is not None:
        state8 = jnp.pad(conv_state, ((0, 0), (SUBLANES - (CONV_W - 1), 0), (0, 0)))
        state8 = state8.reshape(batch * SUBLANES, CONV_DIM)
    xbc = _conv(zx, conv_w, conv_b, state8, batch, seq)
    lv = min(seq, SSD_CHUNK)
    nc = seq // lv
    dt, cum = _dt_prep(zx, dt_bias, a_log, lv)
    dt = dt[:, :SSD_HEADS].reshape(m, SSD_GROUPS, HEADS_PER_GROUP)
    cum = cum[:, :SSD_HEADS].reshape(m, SSD_GROUPS, HEADS_PER_GROUP)
    dtc = jnp.concatenate([dt, cum], axis=-1).transpose(1, 0, 2)
    cumt = cum.reshape(batch * nc, lv, SSD_GROUPS, HEADS_PER_GROUP).transpose(2, 0, 3, 1)
    cumt = jnp.pad(cumt, ((0, 0), (0, 0), (0, SUBLANES - HEADS_PER_GROUP), (0, SSD_CHUNK - lv)))
    h0 = None
    if ssm_state is not None:
        h0 = ssm_state.reshape(batch * SSD_GROUPS, GROUP_W, SSD_STATE)
    yn, h_new = _scan(xbc, zx, dtc, cumt, d_skip, norm_w, h0, batch, seq)
    mix = _mm(yn, w_out_bf, 1024, D_MODEL, "ssd_out_proj")
    raw = zx.reshape(batch, seq, IN_PAD)[:, :, D_INNER:D_INNER + CONV_DIM]
    if conv_state is None:
        new_conv = raw[:, seq - (CONV_W - 1):]
    else:
        new_conv = jnp.concatenate([conv_state, raw], axis=1)[:, -(CONV_W - 1):]
    new_ssm = h_new.reshape(batch, SSD_HEADS, SSD_HEAD_DIM, SSD_STATE)
    return mix, new_conv, new_ssm


def _pool_kernel(x_ref, w_ref, sc_ref, o_ref, *, nseq, lp, lout, start):
    g = pl.program_id(1)
    x = x_ref[...]
    t = lax.broadcasted_iota(jnp.int32, (nseq, lp, POOL_GROUP), 1).reshape(nseq * lp, POOL_GROUP)
    sums = []
    s = x
    k = 1
    while k < POOL_MAX:
        s = s + jnp.where(t >= k, pltpu.roll(s, k, 0), 0.0)
        sums.append(s)
        k *= 2
    win = sums[-1]
    for i in range(len(sums) - 2, -1, -1):
        win = jnp.where(g == i, sums[i], win)
    width = jnp.left_shift(2, g)
    pos = start + t - (lp - lout)
    count = jnp.minimum(pos + 1, width).astype(F32)
    v = win / count - x
    if lout < lp:
        v = v.reshape(nseq, lp, POOL_GROUP)[:, lp - lout:, :].reshape(nseq * lout, POOL_GROUP)
    out = jnp.dot(v.astype(BF16), w_ref[0].astype(BF16), preferred_element_type=F32)
    o_ref[...] = out * sc_ref[...]


def _pool(xcat, pool_w, pool_scale, batch, lp, lout, start):
    nseq = 1 if lp == lout else min(batch, 16)
    return pl.pallas_call(
        functools.partial(_pool_kernel, nseq=nseq, lp=lp, lout=lout, start=start),
        grid=(batch // nseq, len(POOL_WINDOWS)),
        in_specs=[pl.BlockSpec((nseq * lp, POOL_GROUP), lambda i, g: (i, g)),
                  pl.BlockSpec((1, POOL_GROUP, POOL_GROUP), lambda i, g: (g, 0, 0)),
                  pl.BlockSpec((1, POOL_GROUP), lambda i, g: (0, g))],
        out_specs=pl.BlockSpec((nseq * lout, POOL_GROUP), lambda i, g: (i, g)),
        out_shape=jax.ShapeDtypeStruct((batch * lout, D_MODEL), F32),
        compiler_params=_cparams(("parallel", "parallel")),
        name="pool_mixer",
    )(xcat, pool_w, pool_scale.reshape(1, D_MODEL))


def _pool_mixer(x2, batch, seq, pool_state, start, pool_w, pool_scale):
    x3 = x2.reshape(batch, seq, D_MODEL)
    if pool_state is None:
        mix = _pool(x2, pool_w, pool_scale, batch, seq, seq, start)
        new_pool = x3[:, seq - (POOL_MAX - 1):]
    else:
        lp = POOL_MAX + seq
        xcat = jnp.concatenate([jnp.zeros((batch, 1, D_MODEL), F32), pool_state, x3], axis=1)
        mix = _pool(xcat.reshape(batch * lp, D_MODEL), pool_w, pool_scale, batch, lp, seq, start)
        new_pool = xcat[:, lp - (POOL_MAX - 1):]
    return mix, new_pool


_CAND_ROWS = 80


def _kth_largest(v, k):
    cur = v
    m = None
    for it in range(k):
        m = jnp.max(cur, axis=0, keepdims=True)
        if it + 1 < k:
            cur = jnp.where(cur == m, -jnp.inf, cur)
    return m


def _peer_topk_kernel(ht_ref, wq_ref, keys_ref, c1_ref, s2_ref, th_ref, a_scr, b_scr, cand_scr):
    q = jnp.dot(wq_ref[...], ht_ref[...], preferred_element_type=F32)
    s1 = jnp.dot(keys_ref[0, 0], q[:PEER_HALF].astype(BF16), preferred_element_type=F32)
    s2 = jnp.dot(keys_ref[0, 1], q[PEER_HALF:].astype(BF16), preferred_element_type=F32)

    def sorted_top(s, scr):
        cur = s
        for k in range(PEER_TOPK):
            m = jnp.max(cur, axis=0, keepdims=True)
            scr[k:k + 1, :] = m
            cur = jnp.where(cur == m, -jnp.inf, cur)

    sorted_top(s1, a_scr)
    sorted_top(s2, b_scr)
    mx = a_scr[0:1, :] + b_scr[0:1, :]
    tcols = s1.shape[1]
    sub = lax.broadcasted_iota(jnp.int32, (SUBLANES, tcols), 0)

    def fill(shift):
        def a_row(k):
            return (a_scr[k:k + 1, :] - mx) - shift
        cand_scr[0:16, :] = a_row(0) + b_scr[0:16, :]
        for k in range(1, 8):
            keep = PEER_TOPK // (k + 1)
            c = a_row(k) + b_scr[0:8, :]
            if keep < SUBLANES:
                c = jnp.where(sub < keep, c, -jnp.inf)
            cand_scr[8 + 8 * k:16 + 8 * k, :] = c
        cand_scr[72:80, :] = ((a_scr[8:16, :] - mx) - shift) + b_scr[0:1, :]

    fill(0.0)
    cand = cand_scr[...]
    th0 = _kth_largest(cand, PEER_TOPK)
    zsum = jnp.sum(jnp.where(cand >= th0, jnp.exp(cand), 0.0), axis=0, keepdims=True)
    lz = jnp.log(zsum)
    fill(lz)
    th_ref[0] = _kth_largest(cand_scr[...], PEER_TOPK)
    c1_ref[0] = (s1 - mx) - lz
    s2_ref[0] = s2


def _peer_topk(ht_bf, wq_t_bf, keys_bf, tb):
    d, t = ht_bf.shape
    tb = min(tb, t)
    qd = PEER_QUERY_DIM
    return pl.pallas_call(
        _peer_topk_kernel,
        grid=(t // tb, PEER_HEADS),
        in_specs=[pl.BlockSpec((d, tb), lambda i, h: (0, i)),
                  pl.BlockSpec((qd, d), lambda i, h: (h, 0)),
                  pl.BlockSpec((1, 2, PEER_N_KEYS, PEER_HALF), lambda i, h: (h, 0, 0, 0))],
        out_specs=[pl.BlockSpec((1, PEER_N_KEYS, tb), lambda i, h: (h, 0, i)),
                   pl.BlockSpec((1, PEER_N_KEYS, tb), lambda i, h: (h, 0, i)),
                   pl.BlockSpec((1, 1, tb), lambda i, h: (h, 0, i))],
        out_shape=[jax.ShapeDtypeStruct((PEER_HEADS, PEER_N_KEYS, t), F32),
                   jax.ShapeDtypeStruct((PEER_HEADS, PEER_N_KEYS, t), F32),
                   jax.ShapeDtypeStruct((PEER_HEADS, 1, t), F32)],
        scratch_shapes=[pltpu.VMEM((PEER_TOPK, tb), F32), pltpu.VMEM((PEER_TOPK, tb), F32),
                        pltpu.VMEM((_CAND_ROWS, tb), F32)],
        compiler_params=_cparams(("parallel", "parallel")),
        name="peer_topk",
    )(ht_bf, wq_t_bf, keys_bf)


def _gelu(x):
    return 0.5 * x * (1.0 + lax.erf(x * (1.0 / math.sqrt(2.0))))


def _peer_main_kernel(htb_ref, ht_ref, u_ref, vt_ref, c1_ref, s2_ref, th_ref, g_ref, b_ref,
                      o_ref, acc_ref, a_ref, *, eb, tc):
    e = pl.program_id(1)
    ne = pl.num_programs(1)

    @pl.when(e == 0)
    def _():
        acc_ref[...] = jnp.zeros_like(acc_ref)

    tb = acc_ref.shape[1]
    s_all = jnp.dot(u_ref[...], htb_ref[...], preferred_element_type=F32)
    nib = eb // PEER_N_KEYS
    for ii in range(nib):
        i = e * nib + ii
        for c in range(tb // tc):
            cs = slice(c * tc, (c + 1) * tc)
            gate = jnp.zeros((PEER_N_KEYS, tc), F32)
            for h in range(PEER_HEADS):
                w = c1_ref[h, pl.ds(i, 1), cs] + s2_ref[h, :, cs]
                gate = gate + jnp.where(w >= th_ref[h, :, cs], jnp.exp(w), 0.0)
            s = s_all[ii * PEER_N_KEYS:(ii + 1) * PEER_N_KEYS, cs]
            a_ref[ii * PEER_N_KEYS:(ii + 1) * PEER_N_KEYS, cs] = (_gelu(s) * gate).astype(BF16)
    acc_ref[...] += jnp.dot(vt_ref[...], a_ref[...], preferred_element_type=F32)

    @pl.when(e == ne - 1)
    def _():
        v = DEEPNORM_ALPHA * ht_ref[...] + acc_ref[...]
        mu = jnp.mean(v, axis=0, keepdims=True)
        d = v - mu
        var = jnp.mean(d * d, axis=0, keepdims=True)
        o_ref[...] = d * lax.rsqrt(var + LN_EPS) * g_ref[...] + b_ref[...]


def _peer_main(ht_bf, ht, u_bf, vt_bf, c1, s2, th, ln_g, ln_b, tb, eb, tc=256):
    d, t = ht.shape
    tb = min(tb, t)
    tc = min(tc, tb)
    ne = PEER_EXPERTS // eb
    tok = lambda i, e: (0, i)
    return pl.pallas_call(
        functools.partial(_peer_main_kernel, eb=eb, tc=tc),
        grid=(t // tb, ne),
        in_specs=[pl.BlockSpec((d, tb), tok),
                  pl.BlockSpec((d, tb), tok),
                  pl.BlockSpec((eb, d), lambda i, e: (e, 0)),
                  pl.BlockSpec((d, eb), lambda i, e: (0, e)),
                  pl.BlockSpec((PEER_HEADS, PEER_N_KEYS, tb), lambda i, e: (0, 0, i)),
                  pl.BlockSpec((PEER_HEADS, PEER_N_KEYS, tb), lambda i, e: (0, 0, i)),
                  pl.BlockSpec((PEER_HEADS, 1, tb), lambda i, e: (0, 0, i)),
                  pl.BlockSpec((d, 1), lambda i, e: (0, 0)),
                  pl.BlockSpec((d, 1), lambda i, e: (0, 0))],
        out_specs=pl.BlockSpec((d, tb), tok),
        out_shape=jax.ShapeDtypeStruct((d, t), F32),
        scratch_shapes=[pltpu.VMEM((d, tb), F32), pltpu.VMEM((eb, tb), BF16)],
        compiler_params=_cparams(("parallel", "arbitrary")),
        name="peer_main",
    )(ht_bf, ht, u_bf, vt_bf, c1, s2, th, ln_g.reshape(d, 1), ln_b.reshape(d, 1))


def _ple_kernel(h_ref, p_ref, wp_ref, wg_ref, bg_ref, o_ref):
    h = h_ref[...]
    gate = jnp.dot(h.astype(BF16), wg_ref[...], preferred_element_type=F32) + bg_ref[...]
    emb = jnp.dot(p_ref[...].astype(BF16), wp_ref[...], preferred_element_type=F32)
    o_ref[...] = h + emb * _sigmoid(gate)


def _ple(h, p, wp_bf, wg_bf, bg, tm=512):
    m, d = h.shape
    tm = min(tm, m)
    return pl.pallas_call(
        _ple_kernel,
        grid=(m // tm,),
        in_specs=[pl.BlockSpec((tm, d), lambda i: (i, 0)),
                  pl.BlockSpec((tm, PLE_DIM), lambda i: (i, 0)),
                  pl.BlockSpec((PLE_DIM, d), lambda i: (0, 0)),
                  pl.BlockSpec((d, d), lambda i: (0, 0)),
                  pl.BlockSpec((1, d), lambda i: (0, 0))],
        out_specs=pl.BlockSpec((tm, d), lambda i: (i, 0)),
        out_shape=jax.ShapeDtypeStruct((m, d), F32),
        compiler_params=_cparams(("parallel",)),
        name="ple",
    )(h, p, wp_bf, wg_bf, bg.reshape(1, d))


PEER_TB = 512
PEER_EB = 512


def _prep_weights(prm):
    w = dict(prm)
    w_in = jnp.pad(prm['ssd_w_in'], ((0, 0), (0, 0), (0, IN_PAD - IN_DIM)))
    w['w_in_bf'] = w_in.astype(BF16)
    w['w_out_bf'] = prm['ssd_w_out'].astype(BF16)
    w['wq_t_bf'] = jnp.swapaxes(prm['peer_w_q'], 1, 2).astype(BF16)
    w['keys_bf'] = prm['peer_keys'].astype(BF16)
    w['u_bf'] = prm['peer_u'].astype(BF16)
    w['vt_bf'] = jnp.swapaxes(prm['peer_v'], 1, 2).astype(BF16)
    w['wp_bf'] = prm['ple_w'].astype(BF16)
    w['wg_bf'] = prm['ple_gate_w'].astype(BF16)
    return w


def _trunk(x, p, ssm_st, conv_st, pool_st, start, w):
    batch, seq, _ = x.shape
    m = batch * seq
    x2 = x.reshape(m, D_MODEL)
    new_ssm, new_conv, new_pool = [], [], []
    for i in range(DEPTH):
        j = i // N_MIXERS
        if i % N_MIXERS == 0:
            mix, c_new, s_new = _ssd_mixer(
                x2, batch, seq,
                None if conv_st is None else conv_st[j],
                None if ssm_st is None else ssm_st[j],
                w['w_in_bf'][j], w['ssd_conv_w'][j], w['ssd_conv_b'][j], w['ssd_dt_bias'][j],
                w['ssd_a_log'][j], w['ssd_d'][j], w['ssd_norm_w'][j], w['w_out_bf'][j])
            new_conv.append(c_new)
            new_ssm.append(s_new)
        else:
            mix, p_new = _pool_mixer(x2, batch, seq, None if pool_st is None else pool_st[j], start,
                                     w['pool_w'][j], w['pool_scale'][j])
            new_pool.append(p_new)
        h = _resid_ln(x2, mix, w['ln1_g'][i], w['ln1_b'][i])
        ht = h.T
        ht_bf = ht.astype(BF16)
        c1, s2, th = _peer_topk(ht_bf, w['wq_t_bf'][i], w['keys_bf'][i], PEER_TB)
        h2t = _peer_main(ht_bf, ht, w['u_bf'][i], w['vt_bf'][i], c1, s2, th,
                         w['ln2_g'][i], w['ln2_b'][i], PEER_TB, PEER_EB)
        x2 = _ple(h2t.T, p[i].reshape(m, PLE_DIM), w['wp_bf'][i], w['wg_bf'][i], w['ple_gate_b'][i])
    return (x2.reshape(batch, seq, D_MODEL), jnp.stack(new_ssm), jnp.stack(new_conv),
            jnp.stack(new_pool))


def kernel(x_prompt, x_sample, p_prompt, p_sample, state_ssm, state_conv, state_pool,
           ssd_w_in, ssd_conv_w, ssd_conv_b, ssd_dt_bias, ssd_a_log, ssd_d, ssd_norm_w, ssd_w_out,
           pool_w, pool_scale, peer_w_q, peer_keys, peer_u, peer_v,
           ln1_g, ln1_b, ln2_g, ln2_b, ple_w, ple_gate_w, ple_gate_b):
    prm = dict(ssd_w_in=ssd_w_in, ssd_conv_w=ssd_conv_w, ssd_conv_b=ssd_conv_b, ssd_dt_bias=ssd_dt_bias,
               ssd_a_log=ssd_a_log, ssd_d=ssd_d, ssd_norm_w=ssd_norm_w, ssd_w_out=ssd_w_out,
               pool_w=pool_w, pool_scale=pool_scale, peer_w_q=peer_w_q, peer_keys=peer_keys,
               peer_u=peer_u, peer_v=peer_v, ln1_g=ln1_g, ln1_b=ln1_b, ln2_g=ln2_g, ln2_b=ln2_b,
               ple_w=ple_w, ple_gate_w=ple_gate_w, ple_gate_b=ple_gate_b)
    w = _prep_weights(prm)
    y_p, ssm_p, conv_p, pool_p = _trunk(x_prompt, p_prompt, None, None, None, 0, w)
    y_s, ssm_s, conv_s, pool_s = _trunk(x_sample, p_sample, state_ssm, state_conv, state_pool,
                                        PAST_LEN, w)
    return (y_p, y_s, ssm_p, conv_p, pool_p, ssm_s, conv_s, pool_s)
```

```python
import functools
import math

import jax
import jax.numpy as jnp
from jax import lax
from jax.experimental import pallas as pl
from jax.experimental.pallas import tpu as pltpu

F32 = jnp.float32
BF16 = jnp.bfloat16

D_MODEL = 1024
DEPTH = 4
N_MIXERS = 2
D_INNER = 2048
SSD_HEAD_DIM = 64
SSD_HEADS = 32
SSD_GROUPS = 8
HEADS_PER_GROUP = 4
SSD_STATE = 128
CONV_W = 4
CONV_DIM = D_INNER + 2 * SSD_GROUPS * SSD_STATE
IN_DIM = D_INNER + CONV_DIM + SSD_HEADS
SSD_CHUNK = 128
POOL_WINDOWS = (2, 4, 8, 16)
POOL_MAX = 16
POOL_GROUP = 256
PEER_HEADS = 8
PEER_N_KEYS = 128
PEER_EXPERTS = PEER_N_KEYS * PEER_N_KEYS
PEER_TOPK = 16
PEER_QUERY_DIM = 256
PEER_HALF = 128
PLE_DIM = 256
DEEPNORM_ALPHA = (2 * DEPTH) ** 0.25
LN_EPS = 1e-5
RMS_EPS = 1e-5
PAST_LEN = 16384

LANES = 128
SUBLANES = 8
VMEM_LIMIT = 48 * 1024 * 1024

IN_PAD = 6272
GROUP_W = HEADS_PER_GROUP * SSD_HEAD_DIM
NEG_BIG = -1e30
LOG2E = 1.4426950408889634


def _cparams(sem):
    return pltpu.CompilerParams(dimension_semantics=sem, vmem_limit_bytes=VMEM_LIMIT)


def _sigmoid(x):
    return 1.0 / (1.0 + jnp.exp(-x))


def _mm_kernel(x_ref, w_ref, o_ref):
    x = x_ref[...].astype(BF16)
    w = w_ref[...].astype(BF16)
    o_ref[...] = jnp.dot(x, w, preferred_element_type=F32)


def _mm(x, w, tm, tn, name):
    m, k = x.shape
    _, n = w.shape
    tm = min(tm, m)
    return pl.pallas_call(
        _mm_kernel,
        grid=(m // tm, n // tn),
        in_specs=[pl.BlockSpec((tm, k), lambda i, j: (i, 0)),
                  pl.BlockSpec((k, tn), lambda i, j: (0, j))],
        out_specs=pl.BlockSpec((tm, tn), lambda i, j: (i, j)),
        out_shape=jax.ShapeDtypeStruct((m, n), F32),
        compiler_params=_cparams(("parallel", "parallel")),
        name=name,
    )(x, w)


def _resid_ln_kernel(x_ref, y_ref, g_ref, b_ref, o_ref):
    v = DEEPNORM_ALPHA * x_ref[...] + y_ref[...]
    mu = jnp.mean(v, axis=-1, keepdims=True)
    d = v - mu
    var = jnp.mean(d * d, axis=-1, keepdims=True)
    o_ref[...] = d * lax.rsqrt(var + LN_EPS) * g_ref[...] + b_ref[...]


def _resid_ln(x, y, g, b, tm=512):
    m, d = x.shape
    tm = min(tm, m)
    row = pl.BlockSpec((tm, d), lambda i: (i, 0))
    vec = pl.BlockSpec((1, d), lambda i: (0, 0))
    return pl.pallas_call(
        _resid_ln_kernel,
        grid=(m // tm,),
        in_specs=[row, row, vec, vec],
        out_specs=row,
        out_shape=jax.ShapeDtypeStruct((m, d), F32),
        compiler_params=_cparams(("parallel",)),
        name="resid_ln",
    )(x, y, g.reshape(1, d), b.reshape(1, d))


def _conv_kernel(*refs, seq, has_state):
    if has_state:
        x_ref, st_ref, w_ref, b_ref, o_ref = refs
    else:
        x_ref, w_ref, b_ref, o_ref = refs
    x = x_ref[...]
    rows = x.shape[0]
    t = lax.broadcasted_iota(jnp.int32, x.shape, 0) & (seq - 1)
    acc = b_ref[...] + w_ref[CONV_W - 1:CONV_W, :] * x
    for k in range(1, CONV_W):
        tap = pltpu.roll(x, k, 0)
        if has_state:
            prev = pltpu.roll(st_ref[...], rows - (SUBLANES - k), 0)
            tap = jnp.where(t >= k, tap, prev)
        else:
            tap = jnp.where(t >= k, tap, 0.0)
        acc = acc + w_ref[CONV_W - 1 - k:CONV_W - k, :] * tap
    o_ref[...] = acc * _sigmoid(acc)


def _conv(zx, conv_w, conv_b, state8, batch, seq, cb=256):
    has_state = state8 is not None
    col0 = D_INNER // cb
    if has_state:
        assert seq == SUBLANES
        bb = min(batch, 32)
        rows = bb * seq
        grid = (batch // bb, CONV_DIM // cb)
    else:
        rows = seq
        grid = (batch, CONV_DIM // cb)
    in_specs = [pl.BlockSpec((rows, cb), lambda i, c: (i, col0 + c))]
    args = [zx]
    if has_state:
        in_specs.append(pl.BlockSpec((rows, cb), lambda i, c: (i, c)))
        args.append(state8)
    in_specs += [pl.BlockSpec((CONV_W, cb), lambda i, c: (0, c)),
                 pl.BlockSpec((1, cb), lambda i, c: (0, c))]
    args += [conv_w, conv_b.reshape(1, CONV_DIM)]
    return pl.pallas_call(
        functools.partial(_conv_kernel, seq=seq, has_state=has_state),
        grid=grid,
        in_specs=in_specs,
        out_specs=pl.BlockSpec((rows, cb), lambda i, c: (i, c)),
        out_shape=jax.ShapeDtypeStruct((batch * seq, CONV_DIM), F32),
        compiler_params=_cparams(("parallel", "parallel")),
        name="ssd_conv",
    )(*args)


def _dt_kernel(raw_ref, bias_ref, alog_ref, dt_ref, cum_ref, *, seg):
    v = raw_ref[...] + bias_ref[...]
    dt = jnp.maximum(v, 0.0) + jnp.log(1.0 + jnp.exp(-jnp.abs(v)))
    da = dt * (-jnp.exp(alog_ref[...]))
    t = lax.broadcasted_iota(jnp.int32, da.shape, 0) & (seg - 1)
    cum = da
    s = 1
    while s < seg:
        cum = cum + jnp.where(t >= s, pltpu.roll(cum, s, 0), 0.0)
        s *= 2
    dt_ref[...] = dt
    cum_ref[...] = cum


def _dt_prep(zx, dt_bias, a_log, seg):
    m = zx.shape[0]
    rows = min(m, 1024)
    pad = LANES - SSD_HEADS
    bias = jnp.pad(dt_bias, (0, pad)).reshape(1, LANES)
    alog = jnp.pad(a_log, (0, pad)).reshape(1, LANES)
    col = (D_INNER + CONV_DIM) // LANES
    blk = pl.BlockSpec((rows, LANES), lambda i: (i, 0))
    vec = pl.BlockSpec((1, LANES), lambda i: (0, 0))
    return pl.pallas_call(
        functools.partial(_dt_kernel, seg=seg),
        grid=(m // rows,),
        in_specs=[pl.BlockSpec((rows, LANES), lambda i: (i, col)), vec, vec],
        out_specs=[blk, blk],
        out_shape=[jax.ShapeDtypeStruct((m, LANES), F32)] * 2,
        compiler_params=_cparams(("parallel",)),
        name="ssd_dt",
    )(zx, bias, alog)


def _expand_heads(cols, shape):
    hd = lax.broadcasted_iota(jnp.int32, shape, len(shape) - 1) // SSD_HEAD_DIM
    out = jnp.broadcast_to(cols[HEADS_PER_GROUP - 1], shape)
    for j in range(HEADS_PER_GROUP - 2, -1, -1):
        out = jnp.where(hd == j, cols[j], out)
    return out


def _scan_kernel(*refs, lv, has_h0):
    if has_h0:
        (x_ref, b_ref, c_ref, z_ref, dtc_ref, cumt_ref, d_ref, nw_ref, h0_ref, yn_ref, h_ref) = refs
    else:
        (x_ref, b_ref, c_ref, z_ref, dtc_ref, cumt_ref, d_ref, nw_ref, yn_ref, h_ref) = refs
    cl = SSD_CHUNK
    chunk = pl.program_id(2)

    @pl.when(chunk == 0)
    def _():
        if has_h0:
            h_ref[...] = h0_ref[...]
        else:
            h_ref[...] = jnp.zeros_like(h_ref)

    def pad_rows(v):
        if lv == cl:
            return v
        return jnp.concatenate([v, jnp.zeros((cl - lv, v.shape[1]), v.dtype)], axis=0)

    xs = pad_rows(x_ref[...])
    bm = pad_rows(b_ref[...])
    cm = pad_rows(c_ref[...])
    dtc = pad_rows(dtc_ref[0])
    cumt = cumt_ref[0, 0]
    h = h_ref[0]

    row = lax.broadcasted_iota(jnp.int32, (cl, cl), 0)
    col = lax.broadcasted_iota(jnp.int32, (cl, cl), 1)
    valid = row >= col
    if lv < cl:
        valid = valid & (row < lv)

    dt_cols = [dtc[:, j:j + 1] for j in range(HEADS_PER_GROUP)]
    cum_cols = [dtc[:, HEADS_PER_GROUP + j:HEADS_PER_GROUP + j + 1] for j in range(HEADS_PER_GROUP)]
    if lv < cl:
        rmask = lax.broadcasted_iota(jnp.int32, (cl, 1), 0) < lv
        dt_cols = [jnp.where(rmask, v, 0.0) for v in dt_cols]
    last_cols = [v[lv - 1:lv, :] for v in cum_cols]

    wide = (cl, GROUP_W)
    dt_e = _expand_heads(dt_cols, wide)
    cum_e = _expand_heads(cum_cols, wide)
    last_e = _expand_heads(last_cols, wide)
    if lv < cl:
        cum_e = jnp.where(lax.broadcasted_iota(jnp.int32, wide, 0) < lv, cum_e, last_e)

    cb = lax.dot_general(cm.astype(BF16), bm.astype(BF16), (((1,), (1,)), ((), ())),
                         preferred_element_type=F32)
    xdt = xs * dt_e
    hd_w = lax.broadcasted_iota(jnp.int32, wide, 1) // SSD_HEAD_DIM
    m_parts, x_parts = [], []
    for j in range(HEADS_PER_GROUP):
        diff = cum_cols[j] - cumt[j:j + 1, :]
        decay = jnp.exp(jnp.where(valid, diff, NEG_BIG))
        m_parts.append((cb * decay).astype(BF16))
        x_parts.append(jnp.where(hd_w == j, xdt, 0.0).astype(BF16))
    m_cat = jnp.concatenate(m_parts, axis=1)
    x_cat = jnp.concatenate(x_parts, axis=0)
    y = jnp.dot(m_cat, x_cat, preferred_element_type=F32)

    y_state = lax.dot_general(cm.astype(BF16), h.astype(BF16), (((1,), (1,)), ((), ())),
                              preferred_element_type=F32)
    y = y + y_state * jnp.exp(cum_e)
    d_cols = [d_ref[0][:, j:j + 1] for j in range(HEADS_PER_GROUP)]
    y = y + _expand_heads(d_cols, (1, GROUP_W)) * xs

    tail_e = jnp.exp(last_e - cum_e) * dt_e
    w = (xs * tail_e).T
    h_upd = jnp.dot(w.astype(BF16), bm.astype(BF16), preferred_element_type=F32)
    hrow = lax.broadcasted_iota(jnp.int32, (GROUP_W, SSD_STATE), 0) // SSD_HEAD_DIM
    scale = jnp.broadcast_to(jnp.exp(last_cols[HEADS_PER_GROUP - 1]), (GROUP_W, SSD_STATE))
    for j in range(HEADS_PER_GROUP - 2, -1, -1):
        scale = jnp.where(hrow == j, jnp.exp(last_cols[j]), scale)
    h_ref[0] = h * scale + h_upd

    z = z_ref[...]
    g = y[:lv] * (z * _sigmoid(z))
    ms = jnp.mean(g * g, axis=-1, keepdims=True)
    yn_ref[...] = g * lax.rsqrt(ms + RMS_EPS) * nw_ref[...]


def _scan(xbc, zx, dtc, cumt, d_skip, norm_w, h0, batch, seq):
    lv = min(seq, SSD_CHUNK)
    nc = seq // lv
    has_h0 = h0 is not None
    g_w = GROUP_W // LANES
    b_col = D_INNER // SSD_STATE
    c_col = b_col + SSD_GROUPS
    rowblk = lambda b, g, c: b * nc + c
    in_specs = [
        pl.BlockSpec((lv, GROUP_W), lambda b, g, c: (rowblk(b, g, c), g)),
        pl.BlockSpec((lv, SSD_STATE), lambda b, g, c: (rowblk(b, g, c), b_col + g)),
        pl.BlockSpec((lv, SSD_STATE), lambda b, g, c: (rowblk(b, g, c), c_col + g)),
        pl.BlockSpec((lv, GROUP_W), lambda b, g, c: (rowblk(b, g, c), g)),
        pl.BlockSpec((1, lv, SUBLANES), lambda b, g, c: (g, rowblk(b, g, c), 0)),
        pl.BlockSpec((1, 1, SUBLANES, SSD_CHUNK), lambda b, g, c: (g, rowblk(b, g, c), 0, 0)),
        pl.BlockSpec((1, 1, HEADS_PER_GROUP), lambda b, g, c: (g, 0, 0)),
        pl.BlockSpec((1, GROUP_W), lambda b, g, c: (0, g)),
    ]
    del g_w
    args = [xbc, xbc, xbc, zx, dtc, cumt,
            d_skip.reshape(SSD_GROUPS, 1, HEADS_PER_GROUP), norm_w.reshape(1, D_INNER)]
    state_spec = pl.BlockSpec((1, GROUP_W, SSD_STATE), lambda b, g, c: (b * SSD_GROUPS + g, 0, 0))
    if has_h0:
        in_specs.append(state_spec)
        args.append(h0)
    return pl.pallas_call(
        functools.partial(_scan_kernel, lv=lv, has_h0=has_h0),
        grid=(batch, SSD_GROUPS, nc),
        in_specs=in_specs,
        out_specs=[pl.BlockSpec((lv, GROUP_W), lambda b, g, c: (rowblk(b, g, c), g)), state_spec],
        out_shape=[jax.ShapeDtypeStruct((batch * seq, D_INNER), F32),
                   jax.ShapeDtypeStruct((batch * SSD_GROUPS, GROUP_W, SSD_STATE), F32)],
        compiler_params=_cparams(("parallel", "parallel", "arbitrary")),
        name="ssd_scan",
    )(*args)


def _ssd_mixer(x2, batch, seq, conv_state, ssm_state, w_in_bf, conv_w, conv_b, dt_bias, a_log,
               d_skip, norm_w, w_out_bf):
    m = batch * seq
    zx = _mm(x2, w_in_bf, 1024, 896, "ssd_in_proj")
    state8 = None
    if conv_state is not None:
        state8 = jnp.pad(conv_state, ((0, 0), (SUBLANES - (CONV_W - 1), 0), (0, 0)))
        state8 = state8.reshape(batch * SUBLANES, CONV_DIM)
    xbc = _conv(zx, conv_w, conv_b, state8, batch, seq)
    lv = min(seq, SSD_CHUNK)
    nc = seq // lv
    dt, cum = _dt_prep(zx, dt_bias, a_log, lv)
    dt = dt[:, :SSD_HEADS].reshape(m, SSD_GROUPS, HEADS_PER_GROUP)
    cum = cum[:, :SSD_HEADS].reshape(m, SSD_GROUPS, HEADS_PER_GROUP)
    dtc = jnp.concatenate([dt, cum], axis=-1).transpose(1, 0, 2)
    cumt = cum.reshape(batch * nc, lv, SSD_GROUPS, HEADS_PER_GROUP).transpose(2, 0, 3, 1)
    cumt = jnp.pad(cumt, ((0, 0), (0, 0), (0, SUBLANES - HEADS_PER_GROUP), (0, SSD_CHUNK - lv)))
    h0 = None
    if ssm_state is not None:
        h0 = ssm_state.reshape(batch * SSD_GROUPS, GROUP_W, SSD_STATE)
    yn, h_new = _scan(xbc, zx, dtc, cumt, d_skip, norm_w, h0, batch, seq)
    mix = _mm(yn, w_out_bf, 1024, D_MODEL, "ssd_out_proj")
    raw = zx.reshape(batch, seq, IN_PAD)[:, :, D_INNER:D_INNER + CONV_DIM]
    if conv_state is None:
        new_conv = raw[:, seq - (CONV_W - 1):]
    else:
        new_conv = jnp.concatenate([conv_state, raw], axis=1)[:, -(CONV_W - 1):]
    new_ssm = h_new.reshape(batch, SSD_HEADS, SSD_HEAD_DIM, SSD_STATE)
    return mix, new_conv, new_ssm


def _pool_kernel(x_ref, w_ref, sc_ref, o_ref, *, nseq, lp, lout, start):
    g = pl.program_id(1)
    x = x_ref[...]
    t = lax.broadcasted_iota(jnp.int32, (nseq, lp, POOL_GROUP), 1).reshape(nseq * lp, POOL_GROUP)
    sums = []
    s = x
    k = 1
    while k < POOL_MAX:
        s = s + jnp.where(t >= k, pltpu.roll(s, k, 0), 0.0)
        sums.append(s)
        k *= 2
    win = sums[-1]
    for i in range(len(sums) - 2, -1, -1):
        win = jnp.where(g == i, sums[i], win)
    width = jnp.left_shift(2, g)
    pos = start + t - (lp - lout)
    count = jnp.minimum(pos + 1, width).astype(F32)
    v = win / count - x
    if lout < lp:
        v = v.reshape(nseq, lp, POOL_GROUP)[:, lp - lout:, :].reshape(nseq * lout, POOL_GROUP)
    out = jnp.dot(v.astype(BF16), w_ref[0].astype(BF16), preferred_element_type=F32)
    o_ref[...] = out * sc_ref[...]


def _pool(xcat, pool_w, pool_scale, batch, lp, lout, start):
    nseq = 1 if lp == lout else min(batch, 16)
    return pl.pallas_call(
        functools.partial(_pool_kernel, nseq=nseq, lp=lp, lout=lout, start=start),
        grid=(batch // nseq, len(POOL_WINDOWS)),
        in_specs=[pl.BlockSpec((nseq * lp, POOL_GROUP), lambda i, g: (i, g)),
                  pl.BlockSpec((1, POOL_GROUP, POOL_GROUP), lambda i, g: (g, 0, 0)),
                  pl.BlockSpec((1, POOL_GROUP), lambda i, g: (0, g))],
        out_specs=pl.BlockSpec((nseq * lout, POOL_GROUP), lambda i, g: (i, g)),
        out_shape=jax.ShapeDtypeStruct((batch * lout, D_MODEL), F32),
        compiler_params=_cparams(("parallel", "parallel")),
        name="pool_mixer",
    )(xcat, pool_w, pool_scale.reshape(1, D_MODEL))


def _pool_mixer(x2, batch, seq, pool_state, start, pool_w, pool_scale):
    x3 = x2.reshape(batch, seq, D_MODEL)
    if pool_state is None:
        mix = _pool(x2, pool_w, pool_scale, batch, seq, seq, start)
        new_pool = x3[:, seq - (POOL_MAX - 1):]
    else:
        lp = POOL_MAX + seq
        xcat = jnp.concatenate([jnp.zeros((batch, 1, D_MODEL), F32), pool_state, x3], axis=1)
        mix = _pool(xcat.reshape(batch * lp, D_MODEL), pool_w, pool_scale, batch, lp, seq, start)
        new_pool = xcat[:, lp - (POOL_MAX - 1):]
    return mix, new_pool


_CAND_ROWS = 80


def _kth_largest(v, k):
    cur = v
    m = None
    for it in range(k):
        m = jnp.max(cur, axis=0, keepdims=True)
        if it + 1 < k:
            cur = jnp.where(cur == m, -jnp.inf, cur)
    return m


def _peer_topk_kernel(ht_ref, wq_ref, keys_ref, c1_ref, s2_ref, th_ref, a_scr, b_scr, cand_scr):
    q = jnp.dot(wq_ref[...], ht_ref[...], preferred_element_type=F32)
    s1 = jnp.dot(keys_ref[0, 0], q[:PEER_HALF].astype(BF16), preferred_element_type=F32)
    s2 = jnp.dot(keys_ref[0, 1], q[PEER_HALF:].astype(BF16), preferred_element_type=F32)

    def sorted_top(s, scr):
        cur = s
        for k in range(PEER_TOPK):
            m = jnp.max(cur, axis=0, keepdims=True)
            scr[k:k + 1, :] = m
            cur = jnp.where(cur == m, -jnp.inf, cur)

    sorted_top(s1, a_scr)
    sorted_top(s2, b_scr)
    mx = a_scr[0:1, :] + b_scr[0:1, :]
    tcols = s1.shape[1]
    sub = lax.broadcasted_iota(jnp.int32, (SUBLANES, tcols), 0)

    def scaled(v):
        return ((v - mx) - lz) * LOG2E - 1.0

    def fill(a_of, b_of):
        cand_scr[0:16, :] = a_of(a_scr[0:1, :]) + b_of(b_scr[0:16, :])
        b8 = b_of(b_scr[0:8, :])
        for k in range(1, 8):
            keep = PEER_TOPK // (k + 1)
            c = a_of(a_scr[k:k + 1, :]) + b8
            if keep < SUBLANES:
                c = jnp.where(sub < keep, c, -jnp.inf)
            cand_scr[8 + 8 * k:16 + 8 * k, :] = c
        cand_scr[72:80, :] = a_of(a_scr[8:16, :]) + b_of(b_scr[0:1, :])

    fill(lambda v: v - mx, lambda v: v)
    cand = cand_scr[...]
    th0 = _kth_largest(cand, PEER_TOPK)
    zsum = jnp.sum(jnp.where(cand >= th0, jnp.exp(cand), 0.0), axis=0, keepdims=True)
    lz = jnp.log(zsum)
    fill(scaled, lambda v: v * LOG2E)
    th_ref[0] = _kth_largest(cand_scr[...], PEER_TOPK)
    c1_ref[0] = scaled(s1)
    s2_ref[0] = s2 * LOG2E


def _peer_topk(ht_bf, wq_t_bf, keys_bf, tb):
    d, t = ht_bf.shape
    tb = min(tb, t)
    qd = PEER_QUERY_DIM
    return pl.pallas_call(
        _peer_topk_kernel,
        grid=(t // tb, PEER_HEADS),
        in_specs=[pl.BlockSpec((d, tb), lambda i, h: (0, i)),
                  pl.BlockSpec((qd, d), lambda i, h: (h, 0)),
                  pl.BlockSpec((1, 2, PEER_N_KEYS, PEER_HALF), lambda i, h: (h, 0, 0, 0))],
        out_specs=[pl.BlockSpec((1, PEER_N_KEYS, tb), lambda i, h: (h, 0, i)),
                   pl.BlockSpec((1, PEER_N_KEYS, tb), lambda i, h: (h, 0, i)),
                   pl.BlockSpec((1, 1, tb), lambda i, h: (h, 0, i))],
        out_shape=[jax.ShapeDtypeStruct((PEER_HEADS, PEER_N_KEYS, t), F32),
                   jax.ShapeDtypeStruct((PEER_HEADS, PEER_N_KEYS, t), F32),
                   jax.ShapeDtypeStruct((PEER_HEADS, 1, t), F32)],
        scratch_shapes=[pltpu.VMEM((PEER_TOPK, tb), F32), pltpu.VMEM((PEER_TOPK, tb), F32),
                        pltpu.VMEM((_CAND_ROWS, tb), F32)],
        compiler_params=_cparams(("parallel", "parallel")),
        name="peer_topk",
    )(ht_bf, wq_t_bf, keys_bf)


PEER_SUB = 256


def _peer_main_kernel(htb_ref, ht_ref, u_ref, vt_ref, c1_ref, s2_ref, th_ref, g_ref, b_ref,
                      o_ref, acc_ref, s_even, s_odd, *, eb, ne):
    f = pl.program_id(0)
    e_prev = jnp.maximum(f - 1, 0) % ne

    @pl.when(f == 0)
    def _():
        s_odd[...] = jnp.zeros_like(s_odd)

    @pl.when(e_prev == 0)
    def _():
        acc_ref[...] = jnp.zeros_like(acc_ref)

    tb = acc_ref.shape[1]
    nk = PEER_N_KEYS
    assert eb == SUBLANES * nk
    nsub = eb // PEER_SUB
    row0 = pl.multiple_of(e_prev * SUBLANES, SUBLANES)

    def body(s_write, s_read):
        pending = []

        def stage1_piece(rows, half):
            def emit():
                s_write[rows, half] = jnp.dot(u_ref[rows, :], htb_ref[:, half],
                                              preferred_element_type=F32)
            return emit

        def stage2_piece(out_rows, rows, a_sub):
            def emit():
                acc_ref[out_rows, :] += jnp.dot(vt_ref[out_rows, rows], a_sub,
                                                preferred_element_type=F32)
            return emit

        d = acc_ref.shape[0]
        for k in range(nsub):
            rows = slice(k * PEER_SUB, (k + 1) * PEER_SUB)
            a_rows = []
            for ii in range(PEER_SUB // nk):
                r = k * (PEER_SUB // nk) + ii
                a_tiles = []
                for c in range(tb // LANES):
                    cs = slice(c * LANES, (c + 1) * LANES)
                    gate = None
                    for h in range(PEER_HEADS):
                        c1g = c1_ref[h, pl.ds(row0, SUBLANES), cs]
                        w = c1g[r:r + 1, :] + s2_ref[h, :, cs]
                        g = jnp.where(w >= th_ref[h, :, cs], jnp.exp2(w), 0.0)
                        gate = g if gate is None else gate + g
                    s = s_read[r * nk:(r + 1) * nk, cs]
                    act = s * (1.0 + lax.erf(s * (1.0 / math.sqrt(2.0))))
                    a_tiles.append((act * gate).astype(BF16))
                    if pending:
                        pending.pop(0)()
                a_rows.append(jnp.concatenate(a_tiles, axis=1))
                half = slice(ii * (tb // 2), (ii + 1) * (tb // 2))
                pending.append(stage1_piece(rows, half))
            a_sub = jnp.concatenate(a_rows, axis=0)
            for ob in range(d // PEER_SUB):
                pending.append(stage2_piece(slice(ob * PEER_SUB, (ob + 1) * PEER_SUB), rows, a_sub))
        for emit in pending:
            emit()

    @pl.when(f % 2 == 0)
    def _():
        body(s_even, s_odd)

    @pl.when(f % 2 == 1)
    def _():
        body(s_odd, s_even)

    @pl.when((e_prev == ne - 1) & (f > 0))
    def _():
        v = DEEPNORM_ALPHA * ht_ref[...] + acc_ref[...]
        mu = jnp.mean(v, axis=0, keepdims=True)
        d = v - mu
        var = jnp.mean(d * d, axis=0, keepdims=True)
        o_ref[...] = d * lax.rsqrt(var + LN_EPS) * g_ref[...] + b_ref[...]


def _peer_main(ht_bf, ht, u_bf, vt_bf, c1, s2, th, ln_g, ln_b, tb, eb):
    d, t = ht.shape
    tb = min(tb, t)
    ne = PEER_EXPERTS // eb
    nblk = (t // tb) * ne
    cur_t = lambda f: jnp.minimum(f, nblk - 1) // ne
    cur_e = lambda f: jnp.minimum(f, nblk - 1) % ne
    prv_t = lambda f: jnp.maximum(f - 1, 0) // ne
    prv_e = lambda f: jnp.maximum(f - 1, 0) % ne
    return pl.pallas_call(
        functools.partial(_peer_main_kernel, eb=eb, ne=ne),
        grid=(nblk + 1,),
        in_specs=[pl.BlockSpec((d, tb), lambda f: (0, cur_t(f))),
                  pl.BlockSpec((d, tb), lambda f: (0, prv_t(f))),
                  pl.BlockSpec((eb, d), lambda f: (cur_e(f), 0)),
                  pl.BlockSpec((d, eb), lambda f: (0, prv_e(f))),
                  pl.BlockSpec((PEER_HEADS, PEER_N_KEYS, tb), lambda f: (0, 0, prv_t(f))),
                  pl.BlockSpec((PEER_HEADS, PEER_N_KEYS, tb), lambda f: (0, 0, prv_t(f))),
                  pl.BlockSpec((PEER_HEADS, 1, tb), lambda f: (0, 0, prv_t(f))),
                  pl.BlockSpec((d, 1), lambda f: (0, 0)),
                  pl.BlockSpec((d, 1), lambda f: (0, 0))],
        out_specs=pl.BlockSpec((d, tb), lambda f: (0, prv_t(f))),
        out_shape=jax.ShapeDtypeStruct((d, t), F32),
        scratch_shapes=[pltpu.VMEM((d, tb), F32), pltpu.VMEM((eb, tb), F32), pltpu.VMEM((eb, tb), F32)],
        compiler_params=_cparams(("arbitrary",)),
        name="peer_main",
    )(ht_bf, ht, u_bf, vt_bf, c1, s2, th, ln_g.reshape(d, 1), ln_b.reshape(d, 1))


def _ple_kernel(h_ref, p_ref, wp_ref, wg_ref, bg_ref, o_ref):
    h = h_ref[...]
    gate = jnp.dot(h.astype(BF16), wg_ref[...], preferred_element_type=F32) + bg_ref[...]
    emb = jnp.dot(p_ref[...].astype(BF16), wp_ref[...], preferred_element_type=F32)
    o_ref[...] = h + emb * _sigmoid(gate)


def _ple(h, p, wp_bf, wg_bf, bg, tm=512):
    m, d = h.shape
    tm = min(tm, m)
    return pl.pallas_call(
        _ple_kernel,
        grid=(m // tm,),
        in_specs=[pl.BlockSpec((tm, d), lambda i: (i, 0)),
                  pl.BlockSpec((tm, PLE_DIM), lambda i: (i, 0)),
                  pl.BlockSpec((PLE_DIM, d), lambda i: (0, 0)),
                  pl.BlockSpec((d, d), lambda i: (0, 0)),
                  pl.BlockSpec((1, d), lambda i: (0, 0))],
        out_specs=pl.BlockSpec((tm, d), lambda i: (i, 0)),
        out_shape=jax.ShapeDtypeStruct((m, d), F32),
        compiler_params=_cparams(("parallel",)),
        name="ple",
    )(h, p, wp_bf, wg_bf, bg.reshape(1, d))


PEER_TB = 512
PEER_EB = 1024


def _prep_weights(prm):
    w = dict(prm)
    w_in = jnp.pad(prm['ssd_w_in'], ((0, 0), (0, 0), (0, IN_PAD - IN_DIM)))
    w['w_in_bf'] = w_in.astype(BF16)
    w['w_out_bf'] = prm['ssd_w_out'].astype(BF16)
    w['wq_t_bf'] = jnp.swapaxes(prm['peer_w_q'], 1, 2).astype(BF16)
    w['keys_bf'] = prm['peer_keys'].astype(BF16)
    w['u_bf'] = prm['peer_u'].astype(BF16)
    w['vt_bf'] = jnp.swapaxes(prm['peer_v'], 1, 2).astype(BF16)
    w['wp_bf'] = prm['ple_w'].astype(BF16)
    w['wg_bf'] = prm['ple_gate_w'].astype(BF16)
    return w


def _trunk(x, p, ssm_st, conv_st, pool_st, start, w):
    batch, seq, _ = x.shape
    m = batch * seq
    x2 = x.reshape(m, D_MODEL)
    new_ssm, new_conv, new_pool = [], [], []
    for i in range(DEPTH):
        j = i // N_MIXERS
        if i % N_MIXERS == 0:
            mix, c_new, s_new = _ssd_mixer(
                x2, batch, seq,
                None if conv_st is None else conv_st[j],
                None if ssm_st is None else ssm_st[j],
                w['w_in_bf'][j], w['ssd_conv_w'][j], w['ssd_conv_b'][j], w['ssd_dt_bias'][j],
                w['ssd_a_log'][j], w['ssd_d'][j], w['ssd_norm_w'][j], w['w_out_bf'][j])
            new_conv.append(c_new)
            new_ssm.append(s_new)
        else:
            mix, p_new = _pool_mixer(x2, batch, seq, None if pool_st is None else pool_st[j], start,
                                     w['pool_w'][j], w['pool_scale'][j])
            new_pool.append(p_new)
        h = _resid_ln(x2, mix, w['ln1_g'][i], w['ln1_b'][i])
        ht = h.T
        ht_bf = ht.astype(BF16)
        c1, s2, th = _peer_topk(ht_bf, w['wq_t_bf'][i], w['keys_bf'][i], PEER_TB)
        h2t = _peer_main(ht_bf, ht, w['u_bf'][i], w['vt_bf'][i], c1, s2, th,
                         w['ln2_g'][i], w['ln2_b'][i], PEER_TB, PEER_EB)
        x2 = _ple(h2t.T, p[i].reshape(m, PLE_DIM), w['wp_bf'][i], w['wg_bf'][i], w['ple_gate_b'][i])
    return (x2.reshape(batch, seq, D_MODEL), jnp.stack(new_ssm), jnp.stack(new_conv),
            jnp.stack(new_pool))


def kernel(x_prompt, x_sample, p_prompt, p_sample, state_ssm, state_conv, state_pool,
           ssd_w_in, ssd_conv_w, ssd_conv_b, ssd_dt_bias, ssd_a_log, ssd_d, ssd_norm_w, ssd_w_out,
           pool_w, pool_scale, peer_w_q, peer_keys, peer_u, peer_v,
           ln1_g, ln1_b, ln2_g, ln2_b, ple_w, ple_gate_w, ple_gate_b):
    prm = dict(ssd_w_in=ssd_w_in, ssd_conv_w=ssd_conv_w, ssd_conv_b=ssd_conv_b, ssd_dt_bias=ssd_dt_bias,
               ssd_a_log=ssd_a_log, ssd_d=ssd_d, ssd_norm_w=ssd_norm_w, ssd_w_out=ssd_w_out,
               pool_w=pool_w, pool_scale=pool_scale, peer_w_q=peer_w_q, peer_keys=peer_keys,
               peer_u=peer_u, peer_v=peer_v, ln1_g=ln1_g, ln1_b=ln1_b, ln2_g=ln2_g, ln2_b=ln2_b,
               ple_w=ple_w, ple_gate_w=ple_gate_w, ple_gate_b=ple_gate_b)
    w = _prep_weights(prm)
    y_p, ssm_p, conv_p, pool_p = _trunk(x_prompt, p_prompt, None, None, None, 0, w)
    y_s, ssm_s, conv_s, pool_s = _trunk(x_sample, p_sample, state_ssm, state_conv, state_pool,
                                        PAST_LEN, w)
    return (y_p, y_s, ssm_p, conv_p, pool_p, ssm_s, conv_s, pool_s)
```

```python
import functools
import math

import jax
import jax.numpy as jnp
from jax import lax
from jax.experimental import pallas as pl
from jax.experimental.pallas import tpu as pltpu

F32 = jnp.float32
BF16 = jnp.bfloat16

D_MODEL = 1024
DEPTH = 4
N_MIXERS = 2
D_INNER = 2048
SSD_HEAD_DIM = 64
SSD_HEADS = 32
SSD_GROUPS = 8
HEADS_PER_GROUP = 4
SSD_STATE = 128
CONV_W = 4
CONV_DIM = D_INNER + 2 * SSD_GROUPS * SSD_STATE
IN_DIM = D_INNER + CONV_DIM + SSD_HEADS
SSD_CHUNK = 128
POOL_WINDOWS = (2, 4, 8, 16)
POOL_MAX = 16
POOL_GROUP = 256
PEER_HEADS = 8
PEER_N_KEYS = 128
PEER_EXPERTS = PEER_N_KEYS * PEER_N_KEYS
PEER_TOPK = 16
PEER_QUERY_DIM = 256
PEER_HALF = 128
PLE_DIM = 256
DEEPNORM_ALPHA = (2 * DEPTH) ** 0.25
LN_EPS = 1e-5
RMS_EPS = 1e-5
PAST_LEN = 16384

LANES = 128
SUBLANES = 8
VMEM_LIMIT = 48 * 1024 * 1024

IN_PAD = 6272
GROUP_W = HEADS_PER_GROUP * SSD_HEAD_DIM
NEG_BIG = -1e30
LOG2E = 1.4426950408889634


def _cparams(sem):
    return pltpu.CompilerParams(dimension_semantics=sem, vmem_limit_bytes=VMEM_LIMIT)


def _sigmoid(x):
    return 1.0 / (1.0 + jnp.exp(-x))


def _mm_kernel(x_ref, w_ref, o_ref):
    x = x_ref[...].astype(BF16)
    w = w_ref[...].astype(BF16)
    o_ref[...] = jnp.dot(x, w, preferred_element_type=F32)


def _mm(x, w, tm, tn, name):
    m, k = x.shape
    _, n = w.shape
    tm = min(tm, m)
    return pl.pallas_call(
        _mm_kernel,
        grid=(m // tm, n // tn),
        in_specs=[pl.BlockSpec((tm, k), lambda i, j: (i, 0)),
                  pl.BlockSpec((k, tn), lambda i, j: (0, j))],
        out_specs=pl.BlockSpec((tm, tn), lambda i, j: (i, j)),
        out_shape=jax.ShapeDtypeStruct((m, n), F32),
        compiler_params=_cparams(("parallel", "parallel")),
        name=name,
    )(x, w)


def _resid_ln_kernel(x_ref, y_ref, g_ref, b_ref, o_ref):
    v = DEEPNORM_ALPHA * x_ref[...] + y_ref[...]
    mu = jnp.mean(v, axis=-1, keepdims=True)
    d = v - mu
    var = jnp.mean(d * d, axis=-1, keepdims=True)
    o_ref[...] = d * lax.rsqrt(var + LN_EPS) * g_ref[...] + b_ref[...]


def _resid_ln(x, y, g, b, tm=512):
    m, d = x.shape
    tm = min(tm, m)
    row = pl.BlockSpec((tm, d), lambda i: (i, 0))
    vec = pl.BlockSpec((1, d), lambda i: (0, 0))
    return pl.pallas_call(
        _resid_ln_kernel,
        grid=(m // tm,),
        in_specs=[row, row, vec, vec],
        out_specs=row,
        out_shape=jax.ShapeDtypeStruct((m, d), F32),
        compiler_params=_cparams(("parallel",)),
        name="resid_ln",
    )(x, y, g.reshape(1, d), b.reshape(1, d))


def _conv_kernel(*refs, seq, has_state):
    if has_state:
        x_ref, st_ref, w_ref, b_ref, o_ref = refs
    else:
        x_ref, w_ref, b_ref, o_ref = refs
    x = x_ref[...]
    rows = x.shape[0]
    t = lax.broadcasted_iota(jnp.int32, x.shape, 0) & (seq - 1)
    acc = b_ref[...] + w_ref[CONV_W - 1:CONV_W, :] * x
    for k in range(1, CONV_W):
        tap = pltpu.roll(x, k, 0)
        if has_state:
            prev = pltpu.roll(st_ref[...], rows - (SUBLANES - k), 0)
            tap = jnp.where(t >= k, tap, prev)
        else:
            tap = jnp.where(t >= k, tap, 0.0)
        acc = acc + w_ref[CONV_W - 1 - k:CONV_W - k, :] * tap
    o_ref[...] = acc * _sigmoid(acc)


def _conv(zx, conv_w, conv_b, state8, batch, seq, cb=256):
    has_state = state8 is not None
    col0 = D_INNER // cb
    if has_state:
        assert seq == SUBLANES
        bb = min(batch, 32)
        rows = bb * seq
        grid = (batch // bb, CONV_DIM // cb)
    else:
        rows = seq
        grid = (batch, CONV_DIM // cb)
    in_specs = [pl.BlockSpec((rows, cb), lambda i, c: (i, col0 + c))]
    args = [zx]
    if has_state:
        in_specs.append(pl.BlockSpec((rows, cb), lambda i, c: (i, c)))
        args.append(state8)
    in_specs += [pl.BlockSpec((CONV_W, cb), lambda i, c: (0, c)),
                 pl.BlockSpec((1, cb), lambda i, c: (0, c))]
    args += [conv_w, conv_b.reshape(1, CONV_DIM)]
    return pl.pallas_call(
        functools.partial(_conv_kernel, seq=seq, has_state=has_state),
        grid=grid,
        in_specs=in_specs,
        out_specs=pl.BlockSpec((rows, cb), lambda i, c: (i, c)),
        out_shape=jax.ShapeDtypeStruct((batch * seq, CONV_DIM), F32),
        compiler_params=_cparams(("parallel", "parallel")),
        name="ssd_conv",
    )(*args)


def _dt_kernel(raw_ref, bias_ref, alog_ref, dt_ref, cum_ref, *, seg):
    v = raw_ref[...] + bias_ref[...]
    dt = jnp.maximum(v, 0.0) + jnp.log(1.0 + jnp.exp(-jnp.abs(v)))
    da = dt * (-jnp.exp(alog_ref[...]))
    t = lax.broadcasted_iota(jnp.int32, da.shape, 0) & (seg - 1)
    cum = da
    s = 1
    while s < seg:
        cum = cum + jnp.where(t >= s, pltpu.roll(cum, s, 0), 0.0)
        s *= 2
    dt_ref[...] = dt
    cum_ref[...] = cum


def _dt_prep(zx, dt_bias, a_log, seg):
    m = zx.shape[0]
    rows = min(m, 1024)
    pad = LANES - SSD_HEADS
    bias = jnp.pad(dt_bias, (0, pad)).reshape(1, LANES)
    alog = jnp.pad(a_log, (0, pad)).reshape(1, LANES)
    col = (D_INNER + CONV_DIM) // LANES
    blk = pl.BlockSpec((rows, LANES), lambda i: (i, 0))
    vec = pl.BlockSpec((1, LANES), lambda i: (0, 0))
    return pl.pallas_call(
        functools.partial(_dt_kernel, seg=seg),
        grid=(m // rows,),
        in_specs=[pl.BlockSpec((rows, LANES), lambda i: (i, col)), vec, vec],
        out_specs=[blk, blk],
        out_shape=[jax.ShapeDtypeStruct((m, LANES), F32)] * 2,
        compiler_params=_cparams(("parallel",)),
        name="ssd_dt",
    )(zx, bias, alog)


def _expand_heads(cols, shape):
    hd = lax.broadcasted_iota(jnp.int32, shape, len(shape) - 1) // SSD_HEAD_DIM
    out = jnp.broadcast_to(cols[HEADS_PER_GROUP - 1], shape)
    for j in range(HEADS_PER_GROUP - 2, -1, -1):
        out = jnp.where(hd == j, cols[j], out)
    return out


def _scan_kernel(*refs, lv, has_h0):
    if has_h0:
        (x_ref, b_ref, c_ref, z_ref, dtc_ref, cumt_ref, d_ref, nw_ref, h0_ref, yn_ref, h_ref) = refs
    else:
        (x_ref, b_ref, c_ref, z_ref, dtc_ref, cumt_ref, d_ref, nw_ref, yn_ref, h_ref) = refs
    cl = SSD_CHUNK
    chunk = pl.program_id(2)

    @pl.when(chunk == 0)
    def _():
        if has_h0:
            h_ref[...] = h0_ref[...]
        else:
            h_ref[...] = jnp.zeros_like(h_ref)

    def pad_rows(v):
        if lv == cl:
            return v
        return jnp.concatenate([v, jnp.zeros((cl - lv, v.shape[1]), v.dtype)], axis=0)

    xs = pad_rows(x_ref[...])
    bm = pad_rows(b_ref[...])
    cm = pad_rows(c_ref[...])
    dtc = pad_rows(dtc_ref[0])
    cumt = cumt_ref[0, 0]
    h = h_ref[0]

    row = lax.broadcasted_iota(jnp.int32, (cl, cl), 0)
    col = lax.broadcasted_iota(jnp.int32, (cl, cl), 1)
    valid = row >= col
    if lv < cl:
        valid = valid & (row < lv)

    dt_cols = [dtc[:, j:j + 1] for j in range(HEADS_PER_GROUP)]
    cum_cols = [dtc[:, HEADS_PER_GROUP + j:HEADS_PER_GROUP + j + 1] for j in range(HEADS_PER_GROUP)]
    if lv < cl:
        rmask = lax.broadcasted_iota(jnp.int32, (cl, 1), 0) < lv
        dt_cols = [jnp.where(rmask, v, 0.0) for v in dt_cols]
    last_cols = [v[lv - 1:lv, :] for v in cum_cols]

    wide = (cl, GROUP_W)
    dt_e = _expand_heads(dt_cols, wide)
    cum_e = _expand_heads(cum_cols, wide)
    last_e = _expand_heads(last_cols, wide)
    if lv < cl:
        cum_e = jnp.where(lax.broadcasted_iota(jnp.int32, wide, 0) < lv, cum_e, last_e)

    cb = lax.dot_general(cm.astype(BF16), bm.astype(BF16), (((1,), (1,)), ((), ())),
                         preferred_element_type=F32)
    xdt = xs * dt_e
    hd_w = lax.broadcasted_iota(jnp.int32, wide, 1) // SSD_HEAD_DIM
    m_parts, x_parts = [], []
    for j in range(HEADS_PER_GROUP):
        diff = cum_cols[j] - cumt[j:j + 1, :]
        decay = jnp.exp(jnp.where(valid, diff, NEG_BIG))
        m_parts.append((cb * decay).astype(BF16))
        x_parts.append(jnp.where(hd_w == j, xdt, 0.0).astype(BF16))
    m_cat = jnp.concatenate(m_parts, axis=1)
    x_cat = jnp.concatenate(x_parts, axis=0)
    y = jnp.dot(m_cat, x_cat, preferred_element_type=F32)

    y_state = lax.dot_general(cm.astype(BF16), h.astype(BF16), (((1,), (1,)), ((), ())),
                              preferred_element_type=F32)
    y = y + y_state * jnp.exp(cum_e)
    d_cols = [d_ref[0][:, j:j + 1] for j in range(HEADS_PER_GROUP)]
    y = y + _expand_heads(d_cols, (1, GROUP_W)) * xs

    tail_e = jnp.exp(last_e - cum_e) * dt_e
    w = (xs * tail_e).T
    h_upd = jnp.dot(w.astype(BF16), bm.astype(BF16), preferred_element_type=F32)
    hrow = lax.broadcasted_iota(jnp.int32, (GROUP_W, SSD_STATE), 0) // SSD_HEAD_DIM
    scale = jnp.broadcast_to(jnp.exp(last_cols[HEADS_PER_GROUP - 1]), (GROUP_W, SSD_STATE))
    for j in range(HEADS_PER_GROUP - 2, -1, -1):
        scale = jnp.where(hrow == j, jnp.exp(last_cols[j]), scale)
    h_ref[0] = h * scale + h_upd

    z = z_ref[...]
    g = y[:lv] * (z * _sigmoid(z))
    ms = jnp.mean(g * g, axis=-1, keepdims=True)
    yn_ref[...] = g * lax.rsqrt(ms + RMS_EPS) * nw_ref[...]


def _scan(xbc, zx, dtc, cumt, d_skip, norm_w, h0, batch, seq):
    lv = min(seq, SSD_CHUNK)
    nc = seq // lv
    has_h0 = h0 is not None
    g_w = GROUP_W // LANES
    b_col = D_INNER // SSD_STATE
    c_col = b_col + SSD_GROUPS
    rowblk = lambda b, g, c: b * nc + c
    in_specs = [
        pl.BlockSpec((lv, GROUP_W), lambda b, g, c: (rowblk(b, g, c), g)),
        pl.BlockSpec((lv, SSD_STATE), lambda b, g, c: (rowblk(b, g, c), b_col + g)),
        pl.BlockSpec((lv, SSD_STATE), lambda b, g, c: (rowblk(b, g, c), c_col + g)),
        pl.BlockSpec((lv, GROUP_W), lambda b, g, c: (rowblk(b, g, c), g)),
        pl.BlockSpec((1, lv, SUBLANES), lambda b, g, c: (g, rowblk(b, g, c), 0)),
        pl.BlockSpec((1, 1, SUBLANES, SSD_CHUNK), lambda b, g, c: (g, rowblk(b, g, c), 0, 0)),
        pl.BlockSpec((1, 1, HEADS_PER_GROUP), lambda b, g, c: (g, 0, 0)),
        pl.BlockSpec((1, GROUP_W), lambda b, g, c: (0, g)),
    ]
    del g_w
    args = [xbc, xbc, xbc, zx, dtc, cumt,
            d_skip.reshape(SSD_GROUPS, 1, HEADS_PER_GROUP), norm_w.reshape(1, D_INNER)]
    state_spec = pl.BlockSpec((1, GROUP_W, SSD_STATE), lambda b, g, c: (b * SSD_GROUPS + g, 0, 0))
    if has_h0:
        in_specs.append(state_spec)
        args.append(h0)
    return pl.pallas_call(
        functools.partial(_scan_kernel, lv=lv, has_h0=has_h0),
        grid=(batch, SSD_GROUPS, nc),
        in_specs=in_specs,
        out_specs=[pl.BlockSpec((lv, GROUP_W), lambda b, g, c: (rowblk(b, g, c), g)), state_spec],
        out_shape=[jax.ShapeDtypeStruct((batch * seq, D_INNER), F32),
                   jax.ShapeDtypeStruct((batch * SSD_GROUPS, GROUP_W, SSD_STATE), F32)],
        compiler_params=_cparams(("parallel", "parallel", "arbitrary")),
        name="ssd_scan",
    )(*args)


def _ssd_mixer(x2, batch, seq, conv_state, ssm_state, w_in_bf, conv_w, conv_b, dt_bias, a_log,
               d_skip, norm_w, w_out_bf):
    m = batch * seq
    zx = _mm(x2, w_in_bf, 1024, 896, "ssd_in_proj")
    state8 = None
    if conv_state is not None:
        state8 = jnp.pad(conv_state, ((0, 0), (SUBLANES - (CONV_W - 1), 0), (0, 0)))
        state8 = state8.reshape(batch * SUBLANES, CONV_DIM)
    xbc = _conv(zx, conv_w, conv_b, state8, batch, seq)
    lv = min(seq, SSD_CHUNK)
    nc = seq // lv
    dt, cum = _dt_prep(zx, dt_bias, a_log, lv)
    dt = dt[:, :SSD_HEADS].reshape(m, SSD_GROUPS, HEADS_PER_GROUP)
    cum = cum[:, :SSD_HEADS].reshape(m, SSD_GROUPS, HEADS_PER_GROUP)
    dtc = jnp.concatenate([dt, cum], axis=-1).transpose(1, 0, 2)
    cumt = cum.reshape(batch * nc, lv, SSD_GROUPS, HEADS_PER_GROUP).transpose(2, 0, 3, 1)
    cumt = jnp.pad(cumt, ((0, 0), (0, 0), (0, SUBLANES - HEADS_PER_GROUP), (0, SSD_CHUNK - lv)))
    h0 = None
    if ssm_state is not None:
        h0 = ssm_state.reshape(batch * SSD_GROUPS, GROUP_W, SSD_STATE)
    yn, h_new = _scan(xbc, zx, dtc, cumt, d_skip, norm_w, h0, batch, seq)
    mix = _mm(yn, w_out_bf, 1024, D_MODEL, "ssd_out_proj")
    raw = zx.reshape(batch, seq, IN_PAD)[:, :, D_INNER:D_INNER + CONV_DIM]
    if conv_state is None:
        new_conv = raw[:, seq - (CONV_W - 1):]
    else:
        new_conv = jnp.concatenate([conv_state, raw], axis=1)[:, -(CONV_W - 1):]
    new_ssm = h_new.reshape(batch, SSD_HEADS, SSD_HEAD_DIM, SSD_STATE)
    return mix, new_conv, new_ssm


def _pool_kernel(x_ref, w_ref, sc_ref, o_ref, *, nseq, lp, lout, start):
    g = pl.program_id(1)
    x = x_ref[...]
    t = lax.broadcasted_iota(jnp.int32, (nseq, lp, POOL_GROUP), 1).reshape(nseq * lp, POOL_GROUP)
    sums = []
    s = x
    k = 1
    while k < POOL_MAX:
        s = s + jnp.where(t >= k, pltpu.roll(s, k, 0), 0.0)
        sums.append(s)
        k *= 2
    win = sums[-1]
    for i in range(len(sums) - 2, -1, -1):
        win = jnp.where(g == i, sums[i], win)
    width = jnp.left_shift(2, g)
    pos = start + t - (lp - lout)
    count = jnp.minimum(pos + 1, width).astype(F32)
    v = win / count - x
    if lout < lp:
        v = v.reshape(nseq, lp, POOL_GROUP)[:, lp - lout:, :].reshape(nseq * lout, POOL_GROUP)
    out = jnp.dot(v.astype(BF16), w_ref[0].astype(BF16), preferred_element_type=F32)
    o_ref[...] = out * sc_ref[...]


def _pool(xcat, pool_w, pool_scale, batch, lp, lout, start):
    nseq = 1 if lp == lout else min(batch, 16)
    return pl.pallas_call(
        functools.partial(_pool_kernel, nseq=nseq, lp=lp, lout=lout, start=start),
        grid=(batch // nseq, len(POOL_WINDOWS)),
        in_specs=[pl.BlockSpec((nseq * lp, POOL_GROUP), lambda i, g: (i, g)),
                  pl.BlockSpec((1, POOL_GROUP, POOL_GROUP), lambda i, g: (g, 0, 0)),
                  pl.BlockSpec((1, POOL_GROUP), lambda i, g: (0, g))],
        out_specs=pl.BlockSpec((nseq * lout, POOL_GROUP), lambda i, g: (i, g)),
        out_shape=jax.ShapeDtypeStruct((batch * lout, D_MODEL), F32),
        compiler_params=_cparams(("parallel", "parallel")),
        name="pool_mixer",
    )(xcat, pool_w, pool_scale.reshape(1, D_MODEL))


def _pool_mixer(x2, batch, seq, pool_state, start, pool_w, pool_scale):
    x3 = x2.reshape(batch, seq, D_MODEL)
    if pool_state is None:
        mix = _pool(x2, pool_w, pool_scale, batch, seq, seq, start)
        new_pool = x3[:, seq - (POOL_MAX - 1):]
    else:
        lp = POOL_MAX + seq
        xcat = jnp.concatenate([jnp.zeros((batch, 1, D_MODEL), F32), pool_state, x3], axis=1)
        mix = _pool(xcat.reshape(batch * lp, D_MODEL), pool_w, pool_scale, batch, lp, seq, start)
        new_pool = xcat[:, lp - (POOL_MAX - 1):]
    return mix, new_pool


_CAND_ROWS = 80


def _kth_largest(v, k):
    cur = v
    m = None
    for it in range(k):
        m = jnp.max(cur, axis=0, keepdims=True)
        if it + 1 < k:
            cur = jnp.where(cur == m, -jnp.inf, cur)
    return m


def _peer_topk_kernel(ht_ref, wq_ref, keys_ref, c1_ref, s2_ref, th_ref, a_scr, b_scr, cand_scr):
    q = jnp.dot(wq_ref[...], ht_ref[...], preferred_element_type=F32)
    s1 = jnp.dot(keys_ref[0, 0], q[:PEER_HALF].astype(BF16), preferred_element_type=F32)
    s2 = jnp.dot(keys_ref[0, 1], q[PEER_HALF:].astype(BF16), preferred_element_type=F32)

    def sorted_top(s, scr):
        cur = s
        for k in range(PEER_TOPK):
            m = jnp.max(cur, axis=0, keepdims=True)
            scr[k:k + 1, :] = m
            cur = jnp.where(cur == m, -jnp.inf, cur)

    sorted_top(s1, a_scr)
    sorted_top(s2, b_scr)
    mx = a_scr[0:1, :] + b_scr[0:1, :]
    tcols = s1.shape[1]
    sub = lax.broadcasted_iota(jnp.int32, (SUBLANES, tcols), 0)

    def scaled(v):
        return ((v - mx) - lz) * LOG2E - 1.0

    def fill(a_of, b_of):
        cand_scr[0:16, :] = a_of(a_scr[0:1, :]) + b_of(b_scr[0:16, :])
        b8 = b_of(b_scr[0:8, :])
        for k in range(1, 8):
            keep = PEER_TOPK // (k + 1)
            c = a_of(a_scr[k:k + 1, :]) + b8
            if keep < SUBLANES:
                c = jnp.where(sub < keep, c, -jnp.inf)
            cand_scr[8 + 8 * k:16 + 8 * k, :] = c
        cand_scr[72:80, :] = a_of(a_scr[8:16, :]) + b_of(b_scr[0:1, :])

    fill(lambda v: v - mx, lambda v: v)
    cand = cand_scr[...]
    th0 = _kth_largest(cand, PEER_TOPK)
    zsum = jnp.sum(jnp.where(cand >= th0, jnp.exp(cand), 0.0), axis=0, keepdims=True)
    lz = jnp.log(zsum)
    fill(scaled, lambda v: v * LOG2E)
    th_ref[0] = _kth_largest(cand_scr[...], PEER_TOPK)
    c1_ref[0] = scaled(s1)
    s2_ref[0] = s2 * LOG2E


def _peer_topk(ht_bf, wq_t_bf, keys_bf, tb):
    d, t = ht_bf.shape
    tb = min(tb, t)
    qd = PEER_QUERY_DIM
    return pl.pallas_call(
        _peer_topk_kernel,
        grid=(t // tb, PEER_HEADS),
        in_specs=[pl.BlockSpec((d, tb), lambda i, h: (0, i)),
                  pl.BlockSpec((qd, d), lambda i, h: (h, 0)),
                  pl.BlockSpec((1, 2, PEER_N_KEYS, PEER_HALF), lambda i, h: (h, 0, 0, 0))],
        out_specs=[pl.BlockSpec((1, PEER_N_KEYS, tb), lambda i, h: (h, 0, i)),
                   pl.BlockSpec((1, PEER_N_KEYS, tb), lambda i, h: (h, 0, i)),
                   pl.BlockSpec((1, 1, tb), lambda i, h: (h, 0, i))],
        out_shape=[jax.ShapeDtypeStruct((PEER_HEADS, PEER_N_KEYS, t), F32),
                   jax.ShapeDtypeStruct((PEER_HEADS, PEER_N_KEYS, t), F32),
                   jax.ShapeDtypeStruct((PEER_HEADS, 1, t), F32)],
        scratch_shapes=[pltpu.VMEM((PEER_TOPK, tb), F32), pltpu.VMEM((PEER_TOPK, tb), F32),
                        pltpu.VMEM((_CAND_ROWS, tb), F32)],
        compiler_params=_cparams(("parallel", "parallel")),
        name="peer_topk",
    )(ht_bf, wq_t_bf, keys_bf)


PEER_SUB = 256


def _peer_main_kernel(htb_ref, ht_ref, u_ref, vt_ref, c1_ref, s2_ref, th_ref, g_ref, b_ref,
                      o_ref, acc_ref, s_even, s_odd, *, eb, ne):
    f = pl.program_id(0)
    e_prev = jnp.maximum(f - 1, 0) % ne

    @pl.when(f == 0)
    def _():
        s_odd[...] = jnp.zeros_like(s_odd)

    @pl.when(e_prev == 0)
    def _():
        acc_ref[...] = jnp.zeros_like(acc_ref)

    tb = acc_ref.shape[1]
    nk = PEER_N_KEYS
    assert eb == SUBLANES * nk
    nsub = eb // PEER_SUB
    row0 = pl.multiple_of(e_prev * SUBLANES, SUBLANES)

    def body(s_write, s_read):
        pending = []

        def stage1_piece(rows, half):
            def emit():
                s_write[rows, half] = jnp.dot(u_ref[rows, :], htb_ref[:, half],
                                              preferred_element_type=F32)
            return emit

        a_subs = []
        for k in range(nsub):
            rows = slice(k * PEER_SUB, (k + 1) * PEER_SUB)
            a_rows = []
            for ii in range(PEER_SUB // nk):
                r = k * (PEER_SUB // nk) + ii
                a_tiles = []
                for c in range(tb // LANES):
                    cs = slice(c * LANES, (c + 1) * LANES)
                    gate = None
                    for h in range(PEER_HEADS):
                        c1g = c1_ref[h, pl.ds(row0, SUBLANES), cs]
                        w = c1g[r:r + 1, :] + s2_ref[h, :, cs]
                        g = jnp.where(w >= th_ref[h, :, cs], jnp.exp2(w), 0.0)
                        gate = g if gate is None else gate + g
                    s = s_read[r * nk:(r + 1) * nk, cs]
                    act = s * (1.0 + lax.erf(s * (1.0 / math.sqrt(2.0))))
                    a_tiles.append((act * gate).astype(BF16))
                    if pending:
                        pending.pop(0)()
                a_rows.append(jnp.concatenate(a_tiles, axis=1))
                half = slice(ii * (tb // 2), (ii + 1) * (tb // 2))
                pending.append(stage1_piece(rows, half))
            a_subs.append(jnp.concatenate(a_rows, axis=0))
        for emit in pending:
            emit()
        a_all = jnp.concatenate(a_subs, axis=0)
        acc_ref[...] += jnp.dot(vt_ref[...], a_all, preferred_element_type=F32)

    @pl.when(f % 2 == 0)
    def _():
        body(s_even, s_odd)

    @pl.when(f % 2 == 1)
    def _():
        body(s_odd, s_even)

    @pl.when((e_prev == ne - 1) & (f > 0))
    def _():
        v = DEEPNORM_ALPHA * ht_ref[...] + acc_ref[...]
        mu = jnp.mean(v, axis=0, keepdims=True)
        d = v - mu
        var = jnp.mean(d * d, axis=0, keepdims=True)
        o_ref[...] = d * lax.rsqrt(var + LN_EPS) * g_ref[...] + b_ref[...]


def _peer_main(ht_bf, ht, u_bf, vt_bf, c1, s2, th, ln_g, ln_b, tb, eb):
    d, t = ht.shape
    tb = min(tb, t)
    ne = PEER_EXPERTS // eb
    nblk = (t // tb) * ne
    cur_t = lambda f: jnp.minimum(f, nblk - 1) // ne
    cur_e = lambda f: jnp.minimum(f, nblk - 1) % ne
    prv_t = lambda f: jnp.maximum(f - 1, 0) // ne
    prv_e = lambda f: jnp.maximum(f - 1, 0) % ne
    return pl.pallas_call(
        functools.partial(_peer_main_kernel, eb=eb, ne=ne),
        grid=(nblk + 1,),
        in_specs=[pl.BlockSpec((d, tb), lambda f: (0, cur_t(f))),
                  pl.BlockSpec((d, tb), lambda f: (0, prv_t(f))),
                  pl.BlockSpec((eb, d), lambda f: (cur_e(f), 0)),
                  pl.BlockSpec((d, eb), lambda f: (0, prv_e(f))),
                  pl.BlockSpec((PEER_HEADS, PEER_N_KEYS, tb), lambda f: (0, 0, prv_t(f))),
                  pl.BlockSpec((PEER_HEADS, PEER_N_KEYS, tb), lambda f: (0, 0, prv_t(f))),
                  pl.BlockSpec((PEER_HEADS, 1, tb), lambda f: (0, 0, prv_t(f))),
                  pl.BlockSpec((d, 1), lambda f: (0, 0)),
                  pl.BlockSpec((d, 1), lambda f: (0, 0))],
        out_specs=pl.BlockSpec((d, tb), lambda f: (0, prv_t(f))),
        out_shape=jax.ShapeDtypeStruct((d, t), F32),
        scratch_shapes=[pltpu.VMEM((d, tb), F32), pltpu.VMEM((eb, tb), F32), pltpu.VMEM((eb, tb), F32)],
        compiler_params=_cparams(("arbitrary",)),
        name="peer_main",
    )(ht_bf, ht, u_bf, vt_bf, c1, s2, th, ln_g.reshape(d, 1), ln_b.reshape(d, 1))


def _ple_kernel(h_ref, p_ref, wp_ref, wg_ref, bg_ref, o_ref):
    h = h_ref[...]
    gate = jnp.dot(h.astype(BF16), wg_ref[...], preferred_element_type=F32) + bg_ref[...]
    emb = jnp.dot(p_ref[...].astype(BF16), wp_ref[...], preferred_element_type=F32)
    o_ref[...] = h + emb * _sigmoid(gate)


def _ple(h, p, wp_bf, wg_bf, bg, tm=512):
    m, d = h.shape
    tm = min(tm, m)
    return pl.pallas_call(
        _ple_kernel,
        grid=(m // tm,),
        in_specs=[pl.BlockSpec((tm, d), lambda i: (i, 0)),
                  pl.BlockSpec((tm, PLE_DIM), lambda i: (i, 0)),
                  pl.BlockSpec((PLE_DIM, d), lambda i: (0, 0)),
                  pl.BlockSpec((d, d), lambda i: (0, 0)),
                  pl.BlockSpec((1, d), lambda i: (0, 0))],
        out_specs=pl.BlockSpec((tm, d), lambda i: (i, 0)),
        out_shape=jax.ShapeDtypeStruct((m, d), F32),
        compiler_params=_cparams(("parallel",)),
        name="ple",
    )(h, p, wp_bf, wg_bf, bg.reshape(1, d))


PEER_TB = 512
PEER_EB = 1024


def _prep_weights(prm):
    w = dict(prm)
    w_in = jnp.pad(prm['ssd_w_in'], ((0, 0), (0, 0), (0, IN_PAD - IN_DIM)))
    w['w_in_bf'] = w_in.astype(BF16)
    w['w_out_bf'] = prm['ssd_w_out'].astype(BF16)
    w['wq_t_bf'] = jnp.swapaxes(prm['peer_w_q'], 1, 2).astype(BF16)
    w['keys_bf'] = prm['peer_keys'].astype(BF16)
    w['u_bf'] = prm['peer_u'].astype(BF16)
    w['vt_bf'] = jnp.swapaxes(prm['peer_v'], 1, 2).astype(BF16)
    w['wp_bf'] = prm['ple_w'].astype(BF16)
    w['wg_bf'] = prm['ple_gate_w'].astype(BF16)
    return w


def _trunk(x, p, ssm_st, conv_st, pool_st, start, w):
    batch, seq, _ = x.shape
    m = batch * seq
    x2 = x.reshape(m, D_MODEL)
    new_ssm, new_conv, new_pool = [], [], []
    for i in range(DEPTH):
        j = i // N_MIXERS
        if i % N_MIXERS == 0:
            mix, c_new, s_new = _ssd_mixer(
                x2, batch, seq,
                None if conv_st is None else conv_st[j],
                None if ssm_st is None else ssm_st[j],
                w['w_in_bf'][j], w['ssd_conv_w'][j], w['ssd_conv_b'][j], w['ssd_dt_bias'][j],
                w['ssd_a_log'][j], w['ssd_d'][j], w['ssd_norm_w'][j], w['w_out_bf'][j])
            new_conv.append(c_new)
            new_ssm.append(s_new)
        else:
            mix, p_new = _pool_mixer(x2, batch, seq, None if pool_st is None else pool_st[j], start,
                                     w['pool_w'][j], w['pool_scale'][j])
            new_pool.append(p_new)
        h = _resid_ln(x2, mix, w['ln1_g'][i], w['ln1_b'][i])
        ht = h.T
        ht_bf = ht.astype(BF16)
        c1, s2, th = _peer_topk(ht_bf, w['wq_t_bf'][i], w['keys_bf'][i], PEER_TB)
        h2t = _peer_main(ht_bf, ht, w['u_bf'][i], w['vt_bf'][i], c1, s2, th,
                         w['ln2_g'][i], w['ln2_b'][i], PEER_TB, PEER_EB)
        x2 = _ple(h2t.T, p[i].reshape(m, PLE_DIM), w['wp_bf'][i], w['wg_bf'][i], w['ple_gate_b'][i])
    return (x2.reshape(batch, seq, D_MODEL), jnp.stack(new_ssm), jnp.stack(new_conv),
            jnp.stack(new_pool))


def kernel(x_prompt, x_sample, p_prompt, p_sample, state_ssm, state_conv, state_pool,
           ssd_w_in, ssd_conv_w, ssd_conv_b, ssd_dt_bias, ssd_a_log, ssd_d, ssd_norm_w, ssd_w_out,
           pool_w, pool_scale, peer_w_q, peer_keys, peer_u, peer_v,
           ln1_g, ln1_b, ln2_g, ln2_b, ple_w, ple_gate_w, ple_gate_b):
    prm = dict(ssd_w_in=ssd_w_in, ssd_conv_w=ssd_conv_w, ssd_conv_b=ssd_conv_b, ssd_dt_bias=ssd_dt_bias,
               ssd_a_log=ssd_a_log, ssd_d=ssd_d, ssd_norm_w=ssd_norm_w, ssd_w_out=ssd_w_out,
               pool_w=pool_w, pool_scale=pool_scale, peer_w_q=peer_w_q, peer_keys=peer_keys,
               peer_u=peer_u, peer_v=peer_v, ln1_g=ln1_g, ln1_b=ln1_b, ln2_g=ln2_g, ln2_b=ln2_b,
               ple_w=ple_w, ple_gate_w=ple_gate_w, ple_gate_b=ple_gate_b)
    w = _prep_weights(prm)
    y_p, ssm_p, conv_p, pool_p = _trunk(x_prompt, p_prompt, None, None, None, 0, w)
    y_s, ssm_s, conv_s, pool_s = _trunk(x_sample, p_sample, state_ssm, state_conv, state_pool,
                                        PAST_LEN, w)
    return (y_p, y_s, ssm_p, conv_p, pool_p, ssm_s, conv_s, pool_s)
```

```python
import functools
import math

import jax
import jax.numpy as jnp
from jax import lax
from jax.experimental import pallas as pl
from jax.experimental.pallas import tpu as pltpu

F32 = jnp.float32
BF16 = jnp.bfloat16

D_MODEL = 1024
DEPTH = 4
N_MIXERS = 2
D_INNER = 2048
SSD_HEAD_DIM = 64
SSD_HEADS = 32
SSD_GROUPS = 8
HEADS_PER_GROUP = 4
SSD_STATE = 128
CONV_W = 4
CONV_DIM = D_INNER + 2 * SSD_GROUPS * SSD_STATE
IN_DIM = D_INNER + CONV_DIM + SSD_HEADS
SSD_CHUNK = 128
POOL_WINDOWS = (2, 4, 8, 16)
POOL_MAX = 16
POOL_GROUP = 256
PEER_HEADS = 8
PEER_N_KEYS = 128
PEER_EXPERTS = PEER_N_KEYS * PEER_N_KEYS
PEER_TOPK = 16
PEER_QUERY_DIM = 256
PEER_HALF = 128
PLE_DIM = 256
DEEPNORM_ALPHA = (2 * DEPTH) ** 0.25
LN_EPS = 1e-5
RMS_EPS = 1e-5
PAST_LEN = 16384

LANES = 128
SUBLANES = 8
VMEM_LIMIT = 48 * 1024 * 1024

IN_PAD = 6272
GROUP_W = HEADS_PER_GROUP * SSD_HEAD_DIM
NEG_BIG = -1e30
LOG2E = 1.4426950408889634


def _cparams(sem):
    return pltpu.CompilerParams(dimension_semantics=sem, vmem_limit_bytes=VMEM_LIMIT)


def _sigmoid(x):
    return 1.0 / (1.0 + jnp.exp(-x))


def _mm_kernel(x_ref, w_ref, o_ref):
    x = x_ref[...].astype(BF16)
    w = w_ref[...].astype(BF16)
    o_ref[...] = jnp.dot(x, w, preferred_element_type=F32)


def _mm(x, w, tm, tn, name):
    m, k = x.shape
    _, n = w.shape
    tm = min(tm, m)
    return pl.pallas_call(
        _mm_kernel,
        grid=(m // tm, n // tn),
        in_specs=[pl.BlockSpec((tm, k), lambda i, j: (i, 0)),
                  pl.BlockSpec((k, tn), lambda i, j: (0, j))],
        out_specs=pl.BlockSpec((tm, tn), lambda i, j: (i, j)),
        out_shape=jax.ShapeDtypeStruct((m, n), F32),
        compiler_params=_cparams(("parallel", "parallel")),
        name=name,
    )(x, w)


def _resid_ln_kernel(x_ref, y_ref, g_ref, b_ref, o_ref):
    v = DEEPNORM_ALPHA * x_ref[...] + y_ref[...]
    mu = jnp.mean(v, axis=-1, keepdims=True)
    d = v - mu
    var = jnp.mean(d * d, axis=-1, keepdims=True)
    o_ref[...] = d * lax.rsqrt(var + LN_EPS) * g_ref[...] + b_ref[...]


def _resid_ln(x, y, g, b, tm=512):
    m, d = x.shape
    tm = min(tm, m)
    row = pl.BlockSpec((tm, d), lambda i: (i, 0))
    vec = pl.BlockSpec((1, d), lambda i: (0, 0))
    return pl.pallas_call(
        _resid_ln_kernel,
        grid=(m // tm,),
        in_specs=[row, row, vec, vec],
        out_specs=row,
        out_shape=jax.ShapeDtypeStruct((m, d), F32),
        compiler_params=_cparams(("parallel",)),
        name="resid_ln",
    )(x, y, g.reshape(1, d), b.reshape(1, d))


def _conv_kernel(*refs, seq, has_state):
    if has_state:
        x_ref, st_ref, w_ref, b_ref, o_ref = refs
    else:
        x_ref, w_ref, b_ref, o_ref = refs
    x = x_ref[...]
    rows = x.shape[0]
    t = lax.broadcasted_iota(jnp.int32, x.shape, 0) & (seq - 1)
    acc = b_ref[...] + w_ref[CONV_W - 1:CONV_W, :] * x
    for k in range(1, CONV_W):
        tap = pltpu.roll(x, k, 0)
        if has_state:
            prev = pltpu.roll(st_ref[...], rows - (SUBLANES - k), 0)
            tap = jnp.where(t >= k, tap, prev)
        else:
            tap = jnp.where(t >= k, tap, 0.0)
        acc = acc + w_ref[CONV_W - 1 - k:CONV_W - k, :] * tap
    o_ref[...] = acc * _sigmoid(acc)


def _conv(zx, conv_w, conv_b, state8, batch, seq, cb=256):
    has_state = state8 is not None
    col0 = D_INNER // cb
    if has_state:
        assert seq == SUBLANES
        bb = min(batch, 32)
        rows = bb * seq
        grid = (batch // bb, CONV_DIM // cb)
    else:
        rows = seq
        grid = (batch, CONV_DIM // cb)
    in_specs = [pl.BlockSpec((rows, cb), lambda i, c: (i, col0 + c))]
    args = [zx]
    if has_state:
        in_specs.append(pl.BlockSpec((rows, cb), lambda i, c: (i, c)))
        args.append(state8)
    in_specs += [pl.BlockSpec((CONV_W, cb), lambda i, c: (0, c)),
                 pl.BlockSpec((1, cb), lambda i, c: (0, c))]
    args += [conv_w, conv_b.reshape(1, CONV_DIM)]
    return pl.pallas_call(
        functools.partial(_conv_kernel, seq=seq, has_state=has_state),
        grid=grid,
        in_specs=in_specs,
        out_specs=pl.BlockSpec((rows, cb), lambda i, c: (i, c)),
        out_shape=jax.ShapeDtypeStruct((batch * seq, CONV_DIM), F32),
        compiler_params=_cparams(("parallel", "parallel")),
        name="ssd_conv",
    )(*args)


def _dt_kernel(raw_ref, bias_ref, alog_ref, dt_ref, cum_ref, *, seg):
    v = raw_ref[...] + bias_ref[...]
    dt = jnp.maximum(v, 0.0) + jnp.log(1.0 + jnp.exp(-jnp.abs(v)))
    da = dt * (-jnp.exp(alog_ref[...]))
    t = lax.broadcasted_iota(jnp.int32, da.shape, 0) & (seg - 1)
    cum = da
    s = 1
    while s < seg:
        cum = cum + jnp.where(t >= s, pltpu.roll(cum, s, 0), 0.0)
        s *= 2
    dt_ref[...] = dt
    cum_ref[...] = cum


def _dt_prep(zx, dt_bias, a_log, seg):
    m = zx.shape[0]
    rows = min(m, 1024)
    pad = LANES - SSD_HEADS
    bias = jnp.pad(dt_bias, (0, pad)).reshape(1, LANES)
    alog = jnp.pad(a_log, (0, pad)).reshape(1, LANES)
    col = (D_INNER + CONV_DIM) // LANES
    blk = pl.BlockSpec((rows, LANES), lambda i: (i, 0))
    vec = pl.BlockSpec((1, LANES), lambda i: (0, 0))
    return pl.pallas_call(
        functools.partial(_dt_kernel, seg=seg),
        grid=(m // rows,),
        in_specs=[pl.BlockSpec((rows, LANES), lambda i: (i, col)), vec, vec],
        out_specs=[blk, blk],
        out_shape=[jax.ShapeDtypeStruct((m, LANES), F32)] * 2,
        compiler_params=_cparams(("parallel",)),
        name="ssd_dt",
    )(zx, bias, alog)


def _expand_heads(cols, shape):
    hd = lax.broadcasted_iota(jnp.int32, shape, len(shape) - 1) // SSD_HEAD_DIM
    out = jnp.broadcast_to(cols[HEADS_PER_GROUP - 1], shape)
    for j in range(HEADS_PER_GROUP - 2, -1, -1):
        out = jnp.where(hd == j, cols[j], out)
    return out


def _scan_kernel(*refs, lv, has_h0):
    if has_h0:
        (x_ref, b_ref, c_ref, z_ref, dtc_ref, cumt_ref, d_ref, nw_ref, h0_ref, yn_ref, h_ref) = refs
    else:
        (x_ref, b_ref, c_ref, z_ref, dtc_ref, cumt_ref, d_ref, nw_ref, yn_ref, h_ref) = refs
    cl = SSD_CHUNK
    chunk = pl.program_id(2)

    @pl.when(chunk == 0)
    def _():
        if has_h0:
            h_ref[...] = h0_ref[...]
        else:
            h_ref[...] = jnp.zeros_like(h_ref)

    def pad_rows(v):
        if lv == cl:
            return v
        return jnp.concatenate([v, jnp.zeros((cl - lv, v.shape[1]), v.dtype)], axis=0)

    xs = pad_rows(x_ref[...])
    bm = pad_rows(b_ref[...])
    cm = pad_rows(c_ref[...])
    dtc = pad_rows(dtc_ref[0])
    cumt = cumt_ref[0, 0]
    h = h_ref[0]

    row = lax.broadcasted_iota(jnp.int32, (cl, cl), 0)
    col = lax.broadcasted_iota(jnp.int32, (cl, cl), 1)
    valid = row >= col
    if lv < cl:
        valid = valid & (row < lv)

    dt_cols = [dtc[:, j:j + 1] for j in range(HEADS_PER_GROUP)]
    cum_cols = [dtc[:, HEADS_PER_GROUP + j:HEADS_PER_GROUP + j + 1] for j in range(HEADS_PER_GROUP)]
    if lv < cl:
        rmask = lax.broadcasted_iota(jnp.int32, (cl, 1), 0) < lv
        dt_cols = [jnp.where(rmask, v, 0.0) for v in dt_cols]
    last_cols = [v[lv - 1:lv, :] for v in cum_cols]

    wide = (cl, GROUP_W)
    dt_e = _expand_heads(dt_cols, wide)
    cum_e = _expand_heads(cum_cols, wide)
    last_e = _expand_heads(last_cols, wide)
    if lv < cl:
        cum_e = jnp.where(lax.broadcasted_iota(jnp.int32, wide, 0) < lv, cum_e, last_e)

    cb = lax.dot_general(cm.astype(BF16), bm.astype(BF16), (((1,), (1,)), ((), ())),
                         preferred_element_type=F32)
    xdt = xs * dt_e
    hd_w = lax.broadcasted_iota(jnp.int32, wide, 1) // SSD_HEAD_DIM
    m_parts, x_parts = [], []
    for j in range(HEADS_PER_GROUP):
        diff = cum_cols[j] - cumt[j:j + 1, :]
        decay = jnp.exp(jnp.where(valid, diff, NEG_BIG))
        m_parts.append((cb * decay).astype(BF16))
        x_parts.append(jnp.where(hd_w == j, xdt, 0.0).astype(BF16))
    m_cat = jnp.concatenate(m_parts, axis=1)
    x_cat = jnp.concatenate(x_parts, axis=0)
    y = jnp.dot(m_cat, x_cat, preferred_element_type=F32)

    y_state = lax.dot_general(cm.astype(BF16), h.astype(BF16), (((1,), (1,)), ((), ())),
                              preferred_element_type=F32)
    y = y + y_state * jnp.exp(cum_e)
    d_cols = [d_ref[0][:, j:j + 1] for j in range(HEADS_PER_GROUP)]
    y = y + _expand_heads(d_cols, (1, GROUP_W)) * xs

    tail_e = jnp.exp(last_e - cum_e) * dt_e
    w = (xs * tail_e).T
    h_upd = jnp.dot(w.astype(BF16), bm.astype(BF16), preferred_element_type=F32)
    hrow = lax.broadcasted_iota(jnp.int32, (GROUP_W, SSD_STATE), 0) // SSD_HEAD_DIM
    scale = jnp.broadcast_to(jnp.exp(last_cols[HEADS_PER_GROUP - 1]), (GROUP_W, SSD_STATE))
    for j in range(HEADS_PER_GROUP - 2, -1, -1):
        scale = jnp.where(hrow == j, jnp.exp(last_cols[j]), scale)
    h_ref[0] = h * scale + h_upd

    z = z_ref[...]
    g = y[:lv] * (z * _sigmoid(z))
    ms = jnp.mean(g * g, axis=-1, keepdims=True)
    yn_ref[...] = g * lax.rsqrt(ms + RMS_EPS) * nw_ref[...]


SCAN_GPS = 2


def _scan_multi_kernel(*refs, lv, has_h0):
    if has_h0:
        (x_ref, b_ref, c_ref, z_ref, dtc_ref, cumt_ref, d_ref, nw_ref, h0_ref, yn_ref, h_ref) = refs
    else:
        (x_ref, b_ref, c_ref, z_ref, dtc_ref, cumt_ref, d_ref, nw_ref, yn_ref, h_ref) = refs
    for gi in range(SCAN_GPS):
        wide = slice(gi * GROUP_W, (gi + 1) * GROUP_W)
        narrow = slice(gi * SSD_STATE, (gi + 1) * SSD_STATE)
        one = slice(gi, gi + 1)
        views = [x_ref.at[:, wide], b_ref.at[:, narrow], c_ref.at[:, narrow], z_ref.at[:, wide],
                 dtc_ref.at[one], cumt_ref.at[one], d_ref.at[one], nw_ref.at[:, wide]]
        if has_h0:
            views.append(h0_ref.at[one])
        views += [yn_ref.at[:, wide], h_ref.at[one]]
        _scan_kernel(*views, lv=lv, has_h0=has_h0)


def _scan(xbc, zx, dtc, cumt, d_skip, norm_w, h0, batch, seq):
    lv = min(seq, SSD_CHUNK)
    nc = seq // lv
    has_h0 = h0 is not None
    gps = SCAN_GPS
    ngs = SSD_GROUPS // gps
    b_col = D_INNER // (gps * SSD_STATE)
    c_col = b_col + ngs
    rowblk = lambda b, g, c: b * nc + c
    in_specs = [
        pl.BlockSpec((lv, gps * GROUP_W), lambda b, g, c: (rowblk(b, g, c), g)),
        pl.BlockSpec((lv, gps * SSD_STATE), lambda b, g, c: (rowblk(b, g, c), b_col + g)),
        pl.BlockSpec((lv, gps * SSD_STATE), lambda b, g, c: (rowblk(b, g, c), c_col + g)),
        pl.BlockSpec((lv, gps * GROUP_W), lambda b, g, c: (rowblk(b, g, c), g)),
        pl.BlockSpec((gps, lv, SUBLANES), lambda b, g, c: (g, rowblk(b, g, c), 0)),
        pl.BlockSpec((gps, 1, SUBLANES, SSD_CHUNK), lambda b, g, c: (g, rowblk(b, g, c), 0, 0)),
        pl.BlockSpec((gps, 1, HEADS_PER_GROUP), lambda b, g, c: (g, 0, 0)),
        pl.BlockSpec((1, gps * GROUP_W), lambda b, g, c: (0, g)),
    ]
    args = [xbc, xbc, xbc, zx, dtc, cumt,
            d_skip.reshape(SSD_GROUPS, 1, HEADS_PER_GROUP), norm_w.reshape(1, D_INNER)]
    state_spec = pl.BlockSpec((gps, GROUP_W, SSD_STATE), lambda b, g, c: (b * ngs + g, 0, 0))
    if has_h0:
        in_specs.append(state_spec)
        args.append(h0)
    return pl.pallas_call(
        functools.partial(_scan_multi_kernel, lv=lv, has_h0=has_h0),
        grid=(batch, ngs, nc),
        in_specs=in_specs,
        out_specs=[pl.BlockSpec((lv, gps * GROUP_W), lambda b, g, c: (rowblk(b, g, c), g)), state_spec],
        out_shape=[jax.ShapeDtypeStruct((batch * seq, D_INNER), F32),
                   jax.ShapeDtypeStruct((batch * SSD_GROUPS, GROUP_W, SSD_STATE), F32)],
        compiler_params=_cparams(("parallel", "parallel", "arbitrary")),
        name="ssd_scan",
    )(*args)


def _ssd_mixer(x2, batch, seq, conv_state, ssm_state, w_in_bf, conv_w, conv_b, dt_bias, a_log,
               d_skip, norm_w, w_out_bf):
    m = batch * seq
    zx = _mm(x2, w_in_bf, 1024, 896, "ssd_in_proj")
    state8 = None
    if conv_state is not None:
        state8 = jnp.pad(conv_state, ((0, 0), (SUBLANES - (CONV_W - 1), 0), (0, 0)))
        state8 = state8.reshape(batch * SUBLANES, CONV_DIM)
    xbc = _conv(zx, conv_w, conv_b, state8, batch, seq)
    lv = min(seq, SSD_CHUNK)
    nc = seq // lv
    dt, cum = _dt_prep(zx, dt_bias, a_log, lv)
    dt = dt[:, :SSD_HEADS].reshape(m, SSD_GROUPS, HEADS_PER_GROUP)
    cum = cum[:, :SSD_HEADS].reshape(m, SSD_GROUPS, HEADS_PER_GROUP)
    dtc = jnp.concatenate([dt, cum], axis=-1).transpose(1, 0, 2)
    cumt = cum.reshape(batch * nc, lv, SSD_GROUPS, HEADS_PER_GROUP).transpose(2, 0, 3, 1)
    cumt = jnp.pad(cumt, ((0, 0), (0, 0), (0, SUBLANES - HEADS_PER_GROUP), (0, SSD_CHUNK - lv)))
    h0 = None
    if ssm_state is not None:
        h0 = ssm_state.reshape(batch * SSD_GROUPS, GROUP_W, SSD_STATE)
    yn, h_new = _scan(xbc, zx, dtc, cumt, d_skip, norm_w, h0, batch, seq)
    mix = _mm(yn, w_out_bf, 1024, D_MODEL, "ssd_out_proj")
    raw = zx.reshape(batch, seq, IN_PAD)[:, :, D_INNER:D_INNER + CONV_DIM]
    if conv_state is None:
        new_conv = raw[:, seq - (CONV_W - 1):]
    else:
        new_conv = jnp.concatenate([conv_state, raw], axis=1)[:, -(CONV_W - 1):]
    new_ssm = h_new.reshape(batch, SSD_HEADS, SSD_HEAD_DIM, SSD_STATE)
    return mix, new_conv, new_ssm


def _pool_kernel(x_ref, w_ref, sc_ref, o_ref, *, nseq, lp, lout, start):
    g = pl.program_id(1)
    x = x_ref[...]
    t = lax.broadcasted_iota(jnp.int32, (nseq, lp, POOL_GROUP), 1).reshape(nseq * lp, POOL_GROUP)
    sums = []
    s = x
    k = 1
    while k < POOL_MAX:
        s = s + jnp.where(t >= k, pltpu.roll(s, k, 0), 0.0)
        sums.append(s)
        k *= 2
    win = sums[-1]
    for i in range(len(sums) - 2, -1, -1):
        win = jnp.where(g == i, sums[i], win)
    width = jnp.left_shift(2, g)
    pos = start + t - (lp - lout)
    count = jnp.minimum(pos + 1, width).astype(F32)
    v = win / count - x
    if lout < lp:
        v = v.reshape(nseq, lp, POOL_GROUP)[:, lp - lout:, :].reshape(nseq * lout, POOL_GROUP)
    out = jnp.dot(v.astype(BF16), w_ref[0].astype(BF16), preferred_element_type=F32)
    o_ref[...] = out * sc_ref[...]


def _pool(xcat, pool_w, pool_scale, batch, lp, lout, start):
    nseq = 1 if lp == lout else min(batch, 16)
    return pl.pallas_call(
        functools.partial(_pool_kernel, nseq=nseq, lp=lp, lout=lout, start=start),
        grid=(batch // nseq, len(POOL_WINDOWS)),
        in_specs=[pl.BlockSpec((nseq * lp, POOL_GROUP), lambda i, g: (i, g)),
                  pl.BlockSpec((1, POOL_GROUP, POOL_GROUP), lambda i, g: (g, 0, 0)),
                  pl.BlockSpec((1, POOL_GROUP), lambda i, g: (0, g))],
        out_specs=pl.BlockSpec((nseq * lout, POOL_GROUP), lambda i, g: (i, g)),
        out_shape=jax.ShapeDtypeStruct((batch * lout, D_MODEL), F32),
        compiler_params=_cparams(("parallel", "parallel")),
        name="pool_mixer",
    )(xcat, pool_w, pool_scale.reshape(1, D_MODEL))


def _pool_mixer(x2, batch, seq, pool_state, start, pool_w, pool_scale):
    x3 = x2.reshape(batch, seq, D_MODEL)
    if pool_state is None:
        mix = _pool(x2, pool_w, pool_scale, batch, seq, seq, start)
        new_pool = x3[:, seq - (POOL_MAX - 1):]
    else:
        lp = POOL_MAX + seq
        xcat = jnp.concatenate([jnp.zeros((batch, 1, D_MODEL), F32), pool_state, x3], axis=1)
        mix = _pool(xcat.reshape(batch * lp, D_MODEL), pool_w, pool_scale, batch, lp, seq, start)
        new_pool = xcat[:, lp - (POOL_MAX - 1):]
    return mix, new_pool


_CAND_ROWS = 80


def _kth_largest(v, k):
    cur = v
    m = None
    for it in range(k):
        m = jnp.max(cur, axis=0, keepdims=True)
        if it + 1 < k:
            cur = jnp.where(cur == m, -jnp.inf, cur)
    return m


def _peer_topk_kernel(ht_ref, wq_ref, keys_ref, c1_ref, s2_ref, th_ref, a_scr, b_scr, cand_scr):
    q = jnp.dot(wq_ref[...], ht_ref[...], preferred_element_type=F32)
    s1 = jnp.dot(keys_ref[0, 0], q[:PEER_HALF].astype(BF16), preferred_element_type=F32)
    s2 = jnp.dot(keys_ref[0, 1], q[PEER_HALF:].astype(BF16), preferred_element_type=F32)

    def sorted_top(s, scr):
        cur = s
        for k in range(PEER_TOPK):
            m = jnp.max(cur, axis=0, keepdims=True)
            scr[k:k + 1, :] = m
            cur = jnp.where(cur == m, -jnp.inf, cur)

    sorted_top(s1, a_scr)
    sorted_top(s2, b_scr)
    mx = a_scr[0:1, :] + b_scr[0:1, :]
    tcols = s1.shape[1]
    sub = lax.broadcasted_iota(jnp.int32, (SUBLANES, tcols), 0)

    def scaled(v):
        return ((v - mx) - lz) * LOG2E - 1.0

    def fill(a_of, b_of):
        cand_scr[0:16, :] = a_of(a_scr[0:1, :]) + b_of(b_scr[0:16, :])
        b8 = b_of(b_scr[0:8, :])
        for k in range(1, 8):
            keep = PEER_TOPK // (k + 1)
            c = a_of(a_scr[k:k + 1, :]) + b8
            if keep < SUBLANES:
                c = jnp.where(sub < keep, c, -jnp.inf)
            cand_scr[8 + 8 * k:16 + 8 * k, :] = c
        cand_scr[72:80, :] = a_of(a_scr[8:16, :]) + b_of(b_scr[0:1, :])

    fill(lambda v: v - mx, lambda v: v)
    cand = cand_scr[...]
    th0 = _kth_largest(cand, PEER_TOPK)
    zsum = jnp.sum(jnp.where(cand >= th0, jnp.exp(cand), 0.0), axis=0, keepdims=True)
    lz = jnp.log(zsum)
    fill(scaled, lambda v: v * LOG2E)
    th_ref[0] = _kth_largest(cand_scr[...], PEER_TOPK)
    c1_ref[0] = scaled(s1)
    s2_ref[0] = s2 * LOG2E


def _peer_topk(ht_bf, wq_t_bf, keys_bf, tb):
    d, t = ht_bf.shape
    tb = min(tb, t)
    qd = PEER_QUERY_DIM
    return pl.pallas_call(
        _peer_topk_kernel,
        grid=(t // tb, PEER_HEADS),
        in_specs=[pl.BlockSpec((d, tb), lambda i, h: (0, i)),
                  pl.BlockSpec((qd, d), lambda i, h: (h, 0)),
                  pl.BlockSpec((1, 2, PEER_N_KEYS, PEER_HALF), lambda i, h: (h, 0, 0, 0))],
        out_specs=[pl.BlockSpec((1, PEER_N_KEYS, tb), lambda i, h: (h, 0, i)),
                   pl.BlockSpec((1, PEER_N_KEYS, tb), lambda i, h: (h, 0, i)),
                   pl.BlockSpec((1, 1, tb), lambda i, h: (h, 0, i))],
        out_shape=[jax.ShapeDtypeStruct((PEER_HEADS, PEER_N_KEYS, t), F32),
                   jax.ShapeDtypeStruct((PEER_HEADS, PEER_N_KEYS, t), F32),
                   jax.ShapeDtypeStruct((PEER_HEADS, 1, t), F32)],
        scratch_shapes=[pltpu.VMEM((PEER_TOPK, tb), F32), pltpu.VMEM((PEER_TOPK, tb), F32),
                        pltpu.VMEM((_CAND_ROWS, tb), F32)],
        compiler_params=_cparams(("parallel", "parallel")),
        name="peer_topk",
    )(ht_bf, wq_t_bf, keys_bf)


PEER_SUB = 256


def _peer_main_kernel(htb_ref, ht_ref, u_ref, vt_ref, c1_ref, s2_ref, th_ref, g_ref, b_ref,
                      o_ref, acc_ref, s_even, s_odd, *, eb, ne):
    f = pl.program_id(0)
    e_prev = jnp.maximum(f - 1, 0) % ne

    @pl.when(f == 0)
    def _():
        s_odd[...] = jnp.zeros_like(s_odd)

    @pl.when(e_prev == 0)
    def _():
        acc_ref[...] = jnp.zeros_like(acc_ref)

    tb = acc_ref.shape[1]
    nk = PEER_N_KEYS
    assert eb == SUBLANES * nk
    nsub = eb // PEER_SUB
    row0 = pl.multiple_of(e_prev * SUBLANES, SUBLANES)

    def body(s_write, s_read):
        pending = []

        def stage1_piece(rows, half):
            def emit():
                s_write[rows, half] = jnp.dot(u_ref[rows, :], htb_ref[:, half],
                                              preferred_element_type=F32)
            return emit

        a_subs = []
        for k in range(nsub):
            rows = slice(k * PEER_SUB, (k + 1) * PEER_SUB)
            a_rows = []
            for ii in range(PEER_SUB // nk):
                r = k * (PEER_SUB // nk) + ii
                a_tiles = []
                for c in range(tb // LANES):
                    cs = slice(c * LANES, (c + 1) * LANES)
                    gate = None
                    for h in range(PEER_HEADS):
                        c1g = c1_ref[h, pl.ds(row0, SUBLANES), cs]
                        w = c1g[r:r + 1, :] + s2_ref[h, :, cs]
                        g = jnp.where(w >= th_ref[h, :, cs], jnp.exp2(w), 0.0)
                        gate = g if gate is None else gate + g
                    s = s_read[r * nk:(r + 1) * nk, cs]
                    act = s * (1.0 + lax.erf(s * (1.0 / math.sqrt(2.0))))
                    a_tiles.append((act * gate).astype(BF16))
                    if pending:
                        pending.pop(0)()
                a_rows.append(jnp.concatenate(a_tiles, axis=1))
                half = slice(ii * (tb // 2), (ii + 1) * (tb // 2))
                pending.append(stage1_piece(rows, half))
            a_subs.append(jnp.concatenate(a_rows, axis=0))
        for emit in pending:
            emit()
        a_all = jnp.concatenate(a_subs, axis=0)
        acc_ref[...] += jnp.dot(vt_ref[...], a_all, preferred_element_type=F32)

    @pl.when(f % 2 == 0)
    def _():
        body(s_even, s_odd)

    @pl.when(f % 2 == 1)
    def _():
        body(s_odd, s_even)

    @pl.when((e_prev == ne - 1) & (f > 0))
    def _():
        v = DEEPNORM_ALPHA * ht_ref[...] + acc_ref[...]
        mu = jnp.mean(v, axis=0, keepdims=True)
        d = v - mu
        var = jnp.mean(d * d, axis=0, keepdims=True)
        o_ref[...] = d * lax.rsqrt(var + LN_EPS) * g_ref[...] + b_ref[...]


def _peer_main(ht_bf, ht, u_bf, vt_bf, c1, s2, th, ln_g, ln_b, tb, eb):
    d, t = ht.shape
    tb = min(tb, t)
    ne = PEER_EXPERTS // eb
    nblk = (t // tb) * ne
    cur_t = lambda f: jnp.minimum(f, nblk - 1) // ne
    cur_e = lambda f: jnp.minimum(f, nblk - 1) % ne
    prv_t = lambda f: jnp.maximum(f - 1, 0) // ne
    prv_e = lambda f: jnp.maximum(f - 1, 0) % ne
    return pl.pallas_call(
        functools.partial(_peer_main_kernel, eb=eb, ne=ne),
        grid=(nblk + 1,),
        in_specs=[pl.BlockSpec((d, tb), lambda f: (0, cur_t(f))),
                  pl.BlockSpec((d, tb), lambda f: (0, prv_t(f))),
                  pl.BlockSpec((eb, d), lambda f: (cur_e(f), 0)),
                  pl.BlockSpec((d, eb), lambda f: (0, prv_e(f))),
                  pl.BlockSpec((PEER_HEADS, PEER_N_KEYS, tb), lambda f: (0, 0, prv_t(f))),
                  pl.BlockSpec((PEER_HEADS, PEER_N_KEYS, tb), lambda f: (0, 0, prv_t(f))),
                  pl.BlockSpec((PEER_HEADS, 1, tb), lambda f: (0, 0, prv_t(f))),
                  pl.BlockSpec((d, 1), lambda f: (0, 0)),
                  pl.BlockSpec((d, 1), lambda f: (0, 0))],
        out_specs=pl.BlockSpec((d, tb), lambda f: (0, prv_t(f))),
        out_shape=jax.ShapeDtypeStruct((d, t), F32),
        scratch_shapes=[pltpu.VMEM((d, tb), F32), pltpu.VMEM((eb, tb), F32), pltpu.VMEM((eb, tb), F32)],
        compiler_params=_cparams(("arbitrary",)),
        name="peer_main",
    )(ht_bf, ht, u_bf, vt_bf, c1, s2, th, ln_g.reshape(d, 1), ln_b.reshape(d, 1))


def _ple_kernel(h_ref, p_ref, wp_ref, wg_ref, bg_ref, o_ref):
    h = h_ref[...]
    gate = jnp.dot(h.astype(BF16), wg_ref[...], preferred_element_type=F32) + bg_ref[...]
    emb = jnp.dot(p_ref[...].astype(BF16), wp_ref[...], preferred_element_type=F32)
    o_ref[...] = h + emb * _sigmoid(gate)


def _ple(h, p, wp_bf, wg_bf, bg, tm=512):
    m, d = h.shape
    tm = min(tm, m)
    return pl.pallas_call(
        _ple_kernel,
        grid=(m // tm,),
        in_specs=[pl.BlockSpec((tm, d), lambda i: (i, 0)),
                  pl.BlockSpec((tm, PLE_DIM), lambda i: (i, 0)),
                  pl.BlockSpec((PLE_DIM, d), lambda i: (0, 0)),
                  pl.BlockSpec((d, d), lambda i: (0, 0)),
                  pl.BlockSpec((1, d), lambda i: (0, 0))],
        out_specs=pl.BlockSpec((tm, d), lambda i: (i, 0)),
        out_shape=jax.ShapeDtypeStruct((m, d), F32),
        compiler_params=_cparams(("parallel",)),
        name="ple",
    )(h, p, wp_bf, wg_bf, bg.reshape(1, d))


PEER_TB = 512
PEER_EB = 1024


def _prep_weights(prm):
    w = dict(prm)
    w_in = jnp.pad(prm['ssd_w_in'], ((0, 0), (0, 0), (0, IN_PAD - IN_DIM)))
    w['w_in_bf'] = w_in.astype(BF16)
    w['w_out_bf'] = prm['ssd_w_out'].astype(BF16)
    w['wq_t_bf'] = jnp.swapaxes(prm['peer_w_q'], 1, 2).astype(BF16)
    w['keys_bf'] = prm['peer_keys'].astype(BF16)
    w['u_bf'] = prm['peer_u'].astype(BF16)
    w['vt_bf'] = jnp.swapaxes(prm['peer_v'], 1, 2).astype(BF16)
    w['wp_bf'] = prm['ple_w'].astype(BF16)
    w['wg_bf'] = prm['ple_gate_w'].astype(BF16)
    return w


def _trunk(x, p, ssm_st, conv_st, pool_st, start, w):
    batch, seq, _ = x.shape
    m = batch * seq
    x2 = x.reshape(m, D_MODEL)
    new_ssm, new_conv, new_pool = [], [], []
    for i in range(DEPTH):
        j = i // N_MIXERS
        if i % N_MIXERS == 0:
            mix, c_new, s_new = _ssd_mixer(
                x2, batch, seq,
                None if conv_st is None else conv_st[j],
                None if ssm_st is None else ssm_st[j],
                w['w_in_bf'][j], w['ssd_conv_w'][j], w['ssd_conv_b'][j], w['ssd_dt_bias'][j],
                w['ssd_a_log'][j], w['ssd_d'][j], w['ssd_norm_w'][j], w['w_out_bf'][j])
            new_conv.append(c_new)
            new_ssm.append(s_new)
        else:
            mix, p_new = _pool_mixer(x2, batch, seq, None if pool_st is None else pool_st[j], start,
                                     w['pool_w'][j], w['pool_scale'][j])
            new_pool.append(p_new)
        h = _resid_ln(x2, mix, w['ln1_g'][i], w['ln1_b'][i])
        ht = h.T
        ht_bf = ht.astype(BF16)
        c1, s2, th = _peer_topk(ht_bf, w['wq_t_bf'][i], w['keys_bf'][i], PEER_TB)
        h2t = _peer_main(ht_bf, ht, w['u_bf'][i], w['vt_bf'][i], c1, s2, th,
                         w['ln2_g'][i], w['ln2_b'][i], PEER_TB, PEER_EB)
        x2 = _ple(h2t.T, p[i].reshape(m, PLE_DIM), w['wp_bf'][i], w['wg_bf'][i], w['ple_gate_b'][i])
    return (x2.reshape(batch, seq, D_MODEL), jnp.stack(new_ssm), jnp.stack(new_conv),
            jnp.stack(new_pool))


def kernel(x_prompt, x_sample, p_prompt, p_sample, state_ssm, state_conv, state_pool,
           ssd_w_in, ssd_conv_w, ssd_conv_b, ssd_dt_bias, ssd_a_log, ssd_d, ssd_norm_w, ssd_w_out,
           pool_w, pool_scale, peer_w_q, peer_keys, peer_u, peer_v,
           ln1_g, ln1_b, ln2_g, ln2_b, ple_w, ple_gate_w, ple_gate_b):
    prm = dict(ssd_w_in=ssd_w_in, ssd_conv_w=ssd_conv_w, ssd_conv_b=ssd_conv_b, ssd_dt_bias=ssd_dt_bias,
               ssd_a_log=ssd_a_log, ssd_d=ssd_d, ssd_norm_w=ssd_norm_w, ssd_w_out=ssd_w_out,
               pool_w=pool_w, pool_scale=pool_scale, peer_w_q=peer_w_q, peer_keys=peer_keys,
               peer_u=peer_u, peer_v=peer_v, ln1_g=ln1_g, ln1_b=ln1_b, ln2_g=ln2_g, ln2_b=ln2_b,
               ple_w=ple_w, ple_gate_w=ple_gate_w, ple_gate_b=ple_gate_b)
    w = _prep_weights(prm)
    y_p, ssm_p, conv_p, pool_p = _trunk(x_prompt, p_prompt, None, None, None, 0, w)
    y_s, ssm_s, conv_s, pool_s = _trunk(x_sample, p_sample, state_ssm, state_conv, state_pool,
                                        PAST_LEN, w)
    return (y_p, y_s, ssm_p, conv_p, pool_p, ssm_s, conv_s, pool_s)
```
